```python
import math
import jax, jax.numpy as jnp
from jax import lax
import numpy as np


D_MODEL = 1024
BATCH = 8
SEQ = 4096
DEPTH = 4

N_MIXERS = 2
N_SSD = (DEPTH + 1) // 2
N_GMLP = DEPTH // 2

SSD_EXPAND = 2
SSD_INNER = SSD_EXPAND * D_MODEL
SSD_HEADDIM = 64
SSD_HEADS = SSD_INNER // SSD_HEADDIM
SSD_GROUPS = 8
SSD_STATE = 128
SSD_CONV_DIM = SSD_INNER + 2 * SSD_GROUPS * SSD_STATE
SSD_IN_DIM = 2 * SSD_INNER + 2 * SSD_GROUPS * SSD_STATE + SSD_HEADS
CONV_K = 4
CHUNK = 128
DT_MIN = 0.001
DT_MAX = 0.1

GMLP_INNER = 2 * D_MODEL
GMLP_GROUPS = 16
GMLP_GROUP_DIM = GMLP_INNER // GMLP_GROUPS
GMLP_CHUNK = 128

FFN_DIM = int(math.ceil((8 * D_MODEL / 3) / 256) * 256)

PLE_DIM = 256

RMS_EPS = 1e-6
LN_EPS = 1e-5

kernel_name = 'hybrid_ssd_gmlp_trunk'


def rmsnorm(x, w, eps=RMS_EPS):
    xf = x.astype(jnp.float32)
    y = xf * lax.rsqrt(jnp.mean(xf * xf, axis=-1, keepdims=True) + eps)
    return (y * w.astype(jnp.float32)).astype(x.dtype)


def layernorm(x, w, b, eps=LN_EPS):
    xf = x.astype(jnp.float32)
    mu = jnp.mean(xf, axis=-1, keepdims=True)
    xc = xf - mu
    y = xc * lax.rsqrt(jnp.mean(xc * xc, axis=-1, keepdims=True) + eps)
    return (y * w.astype(jnp.float32) + b.astype(jnp.float32)).astype(x.dtype)


def gated_rmsnorm(y, z, w, eps=LN_EPS):
    g = (y * jax.nn.silu(z)).astype(jnp.float32)
    shp = g.shape
    g = g.reshape(shp[:-1] + (SSD_GROUPS, shp[-1] // SSD_GROUPS))
    g = g * lax.rsqrt(jnp.mean(g * g, axis=-1, keepdims=True) + eps)
    return (g.reshape(shp) * w.astype(jnp.float32)).astype(y.dtype)


def causal_dwconv(x, w, b):
    k, c = w.shape
    y = lax.conv_general_dilated(
        x, w[:, None, :].astype(x.dtype), window_strides=(1,), padding=[(k - 1, 0)],
        dimension_numbers=('NWC', 'WIO', 'NWC'), feature_group_count=c)
    return y + b.astype(x.dtype)


def segsum(a):
    t = a.shape[-1]
    cs = jnp.cumsum(a, axis=-1)
    diff = cs[..., :, None] - cs[..., None, :]
    mask = jnp.tril(jnp.ones((t, t), dtype=bool))
    return jnp.where(mask, diff, -jnp.inf)


def ssd_scan(x, dt, a, bm, cm):
    b, s, h, p = x.shape
    g, n = bm.shape[-2], bm.shape[-1]
    r = h // g
    c = s // CHUNK
    dtype = x.dtype
    xr = (x * dt[..., None]).reshape(b, c, CHUNK, g, r, p)
    da = (dt.astype(jnp.float32) * a).reshape(b, c, CHUNK, g, r).transpose(0, 1, 3, 4, 2)
    da_cs = jnp.cumsum(da, axis=-1)
    br = bm.reshape(b, c, CHUNK, g, n)
    cr = cm.reshape(b, c, CHUNK, g, n)
    lmat = jnp.exp(segsum(da)).astype(dtype)
    cb = jnp.einsum('bclgn,bcsgn->bcgls', cr, br)
    wmat = cb[:, :, :, None] * lmat
    y_diag = jnp.einsum('bcgrls,bcsgrp->bclgrp', wmat, xr)
    dec_states = jnp.exp(da_cs[..., -1:] - da_cs).astype(dtype).transpose(0, 1, 4, 2, 3)
    states = jnp.einsum('bclgn,bclgrp->bcgrpn', br, xr * dec_states[..., None])
    chunk_decay = jnp.exp(da_cs[..., -1]).astype(dtype)

    def step(carry, inp):
        dec_c, st_c = inp
        new = dec_c[..., None, None] * carry + st_c
        return new, carry

    init = jnp.zeros((b, g, r, p, n), dtype=states.dtype)
    _, prev = lax.scan(step, init, (jnp.moveaxis(chunk_decay, 1, 0), jnp.moveaxis(states, 1, 0)))
    prev = jnp.moveaxis(prev, 0, 1)
    dec_out = jnp.exp(da_cs).astype(dtype).transpose(0, 1, 4, 2, 3)
    y_off = jnp.einsum('bclgn,bcgrpn->bclgrp', cr, prev) * dec_out[..., None]
    return (y_diag + y_off).reshape(b, s, h, p)


def ssd_mixer(u, w_in, conv_w, conv_b, dt_bias, a_log, d_skip, norm_w, w_out):
    b, s, _ = u.shape
    zxbcdt = u @ w_in
    z = zxbcdt[..., :SSD_INNER]
    xbc = zxbcdt[..., SSD_INNER:SSD_INNER + SSD_CONV_DIM]
    dt = zxbcdt[..., SSD_INNER + SSD_CONV_DIM:]
    xbc = jax.nn.silu(causal_dwconv(xbc, conv_w, conv_b))
    xs = xbc[..., :SSD_INNER].reshape(b, s, SSD_HEADS, SSD_HEADDIM)
    bm = xbc[..., SSD_INNER:SSD_INNER + SSD_GROUPS * SSD_STATE].reshape(b, s, SSD_GROUPS, SSD_STATE)
    cm = xbc[..., SSD_INNER + SSD_GROUPS * SSD_STATE:].reshape(b, s, SSD_GROUPS, SSD_STATE)
    dt = jax.nn.softplus(dt + dt_bias)
    a = -jnp.exp(a_log.astype(jnp.float32))
    y = ssd_scan(xs, dt, a, bm, cm) + xs * d_skip[:, None]
    y = gated_rmsnorm(y.reshape(b, s, SSD_INNER), z, norm_w)
    return y @ w_out


def gmlp_mixer(u, w_in, b_in, ln_w, ln_b, w_s, b_s, w_out):
    b, s, _ = u.shape
    c = s // GMLP_CHUNK
    hp = jax.nn.gelu(u @ w_in + b_in, approximate=False)
    uu = hp[..., :GMLP_INNER]
    vv = layernorm(hp[..., GMLP_INNER:], ln_w, ln_b)
    vv = vv.reshape(b, c, GMLP_CHUNK, GMLP_GROUPS, GMLP_GROUP_DIM)
    ws = jnp.tril(w_s)
    mixed = jnp.einsum('gts,bcsgd->bctgd', ws, vv) + b_s.T[None, None, :, :, None]
    return (uu * mixed.reshape(b, s, GMLP_INNER)) @ w_out


def swiglu(u, w_gate, w_up, w_down):
    return (jax.nn.silu(u @ w_gate) * (u @ w_up)) @ w_down


def setup_inputs(seed: int = 0) -> dict:
    key = jax.random.key(seed)
    ks = jax.random.split(key, 32)
    f32 = jnp.float32

    def nrm(k, shape, scale):
        return jax.random.normal(k, shape, f32) * scale

    x = nrm(ks[0], (BATCH, SEQ, D_MODEL), 1.0)
    p = nrm(ks[1], (DEPTH, BATCH, SEQ, PLE_DIM), 1.0)
    norm_mix = 1.0 + nrm(ks[2], (DEPTH, D_MODEL), 0.02)
    norm_ffn = 1.0 + nrm(ks[3], (DEPTH, D_MODEL), 0.02)
    ssd_w_in = nrm(ks[4], (N_SSD, D_MODEL, SSD_IN_DIM), D_MODEL ** -0.5)
    ssd_conv_w = nrm(ks[5], (N_SSD, CONV_K, SSD_CONV_DIM), CONV_K ** -0.5)
    ssd_conv_b = nrm(ks[6], (N_SSD, SSD_CONV_DIM), 0.02)
    dt0 = jnp.exp(jax.random.uniform(ks[7], (N_SSD, SSD_HEADS), f32)
                  * (math.log(DT_MAX) - math.log(DT_MIN)) + math.log(DT_MIN))
    ssd_dt_bias = dt0 + jnp.log(-jnp.expm1(-dt0))
    ssd_a_log = jnp.log(jax.random.uniform(ks[8], (N_SSD, SSD_HEADS), f32, minval=1.0, maxval=16.0))
    ssd_d = 1.0 + nrm(ks[9], (N_SSD, SSD_HEADS), 0.02)
    ssd_norm_w = 1.0 + nrm(ks[10], (N_SSD, SSD_INNER), 0.02)
    ssd_w_out = nrm(ks[11], (N_SSD, SSD_INNER, D_MODEL), SSD_INNER ** -0.5)
    gmlp_w_in = nrm(ks[12], (N_GMLP, D_MODEL, 2 * GMLP_INNER), D_MODEL ** -0.5)
    gmlp_b_in = nrm(ks[13], (N_GMLP, 2 * GMLP_INNER), 0.02)
    gmlp_ln_w = 1.0 + nrm(ks[14], (N_GMLP, GMLP_INNER), 0.02)
    gmlp_ln_b = nrm(ks[15], (N_GMLP, GMLP_INNER), 0.02)
    gmlp_w_s = nrm(ks[16], (N_GMLP, GMLP_GROUPS, GMLP_CHUNK, GMLP_CHUNK), GMLP_CHUNK ** -0.5)
    gmlp_b_s = 1.0 + nrm(ks[17], (N_GMLP, GMLP_GROUPS, GMLP_CHUNK), 0.02)
    gmlp_w_out = nrm(ks[18], (N_GMLP, GMLP_INNER, D_MODEL), GMLP_INNER ** -0.5)
    ffn_w_gate = nrm(ks[19], (DEPTH, D_MODEL, FFN_DIM), D_MODEL ** -0.5)
    ffn_w_up = nrm(ks[20], (DEPTH, D_MODEL, FFN_DIM), D_MODEL ** -0.5)
    ffn_w_down = nrm(ks[21], (DEPTH, FFN_DIM, D_MODEL), FFN_DIM ** -0.5)
    ple_w_proj = nrm(ks[22], (DEPTH, PLE_DIM, D_MODEL), PLE_DIM ** -0.5)
    ple_norm = 1.0 + nrm(ks[23], (DEPTH, D_MODEL), 0.02)
    ple_gate_norm = 1.0 + nrm(ks[24], (DEPTH, D_MODEL), 0.02)
    ple_w_gate = nrm(ks[25], (DEPTH, D_MODEL, D_MODEL), D_MODEL ** -0.5)
    final_norm = 1.0 + nrm(ks[26], (D_MODEL,), 0.02)
    return {'x': x, 'p': p, 'norm_mix': norm_mix, 'norm_ffn': norm_ffn,
            'ssd_w_in': ssd_w_in, 'ssd_conv_w': ssd_conv_w, 'ssd_conv_b': ssd_conv_b,
            'ssd_dt_bias': ssd_dt_bias, 'ssd_a_log': ssd_a_log, 'ssd_d': ssd_d,
            'ssd_norm_w': ssd_norm_w, 'ssd_w_out': ssd_w_out,
            'gmlp_w_in': gmlp_w_in, 'gmlp_b_in': gmlp_b_in, 'gmlp_ln_w': gmlp_ln_w,
            'gmlp_ln_b': gmlp_ln_b, 'gmlp_w_s': gmlp_w_s, 'gmlp_b_s': gmlp_b_s,
            'gmlp_w_out': gmlp_w_out,
            'ffn_w_gate': ffn_w_gate, 'ffn_w_up': ffn_w_up, 'ffn_w_down': ffn_w_down,
            'ple_w_proj': ple_w_proj, 'ple_norm': ple_norm, 'ple_gate_norm': ple_gate_norm,
            'ple_w_gate': ple_w_gate, 'final_norm': final_norm}


def reference(x, p, norm_mix, norm_ffn,
              ssd_w_in, ssd_conv_w, ssd_conv_b, ssd_dt_bias, ssd_a_log, ssd_d, ssd_norm_w, ssd_w_out,
              gmlp_w_in, gmlp_b_in, gmlp_ln_w, gmlp_ln_b, gmlp_w_s, gmlp_b_s, gmlp_w_out,
              ffn_w_gate, ffn_w_up, ffn_w_down,
              ple_w_proj, ple_norm, ple_gate_norm, ple_w_gate, final_norm):
    h = x
    for i in range(DEPTH):
        j = i // N_MIXERS
        hn = rmsnorm(h, norm_mix[i])
        if i % N_MIXERS == 0:
            mix = ssd_mixer(hn, ssd_w_in[j], ssd_conv_w[j], ssd_conv_b[j], ssd_dt_bias[j],
                            ssd_a_log[j], ssd_d[j], ssd_norm_w[j], ssd_w_out[j])
        else:
            mix = gmlp_mixer(hn, gmlp_w_in[j], gmlp_b_in[j], gmlp_ln_w[j], gmlp_ln_b[j],
                             gmlp_w_s[j], gmlp_b_s[j], gmlp_w_out[j])
        h = h + mix
        h = h + swiglu(rmsnorm(h, norm_ffn[i]), ffn_w_gate[i], ffn_w_up[i], ffn_w_down[i])
        e = rmsnorm(p[i] @ ple_w_proj[i], ple_norm[i])
        gate = jax.nn.sigmoid(rmsnorm(h, ple_gate_norm[i]) @ ple_w_gate[i])
        h = h + gate * e
    return rmsnorm(h, final_norm)
```

```python
import functools
import math

import numpy as np
import jax
import jax.numpy as jnp
from jax import lax
from jax.experimental import pallas as pl
from jax.experimental.pallas import tpu as pltpu

F32 = jnp.float32
BF16 = jnp.bfloat16

DEPTH = 4
N_MIXERS = 2

SSD_HEADDIM = 64
SSD_GROUPS = 8
SSD_STATE = 128
SSD_CHUNK = 128
CONV_K = 4
HEADS_PER_GROUP = 4
GROUP_WIDTH = HEADS_PER_GROUP * SSD_HEADDIM
SSD_HEADS = SSD_GROUPS * HEADS_PER_GROUP
SPLIT_WIDTH = 3 * SSD_HEADS

GMLP_CHUNK = 128
GMLP_GROUP_DIM = 128

RMS_EPS = 1e-6
LN_EPS = 1e-5

CONV_HALO = 8
V7X_VMEM_LIMIT_BYTES = 56 * 1024 * 1024


def _bdot(a, b):
    return jnp.dot(a, b, preferred_element_type=F32)


def _rms(x, w, eps):
    return x * lax.rsqrt(jnp.mean(x * x, axis=-1, keepdims=True) + eps) * w


def _sigmoid(x):
    return 1.0 / (1.0 + jnp.exp(-x))


def _silu(x):
    return x * _sigmoid(x)


def _gelu(x):
    return 0.5 * x * (1.0 + lax.erf(x * (1.0 / math.sqrt(2.0))))


def _softplus(x):
    return jnp.maximum(x, 0.0) + jnp.log1p(jnp.exp(-jnp.abs(x)))


def _bf16_pieces(x):
    hi = x.astype(BF16)
    r1 = x - hi.astype(F32)
    mid = r1.astype(BF16)
    lo = (r1 - mid.astype(F32)).astype(BF16)
    return hi, mid, lo


def _split_select(x3):
    hi, mid, lo = _bf16_pieces(x3)
    lane = lax.broadcasted_iota(jnp.int32, x3.shape, 1)
    return jnp.where(lane < SSD_HEADS, hi, jnp.where(lane < 2 * SSD_HEADS, mid, lo))


def _weight_spec(shape, grid_rank):
    zeros = (0,) * len(shape)
    if grid_rank == 1:
        index_map = lambda i: zeros
    else:
        index_map = lambda b, i: zeros
    return pl.BlockSpec(shape, index_map, pipeline_mode=pl.Buffered(1))


FFN_COL_BLOCK = 256


def _ffn_ple_kernel(h_ref, p_ref, nf_ref, wg_ref, wu_ref, wd_ref, wp_ref, pn_ref,
                    gn_ref, wpg_ref, fn_ref, o_ref, a_scr, *, final):
    h = h_ref[...]
    hn = _rms(h, nf_ref[...], RMS_EPS).astype(BF16)
    ffn_dim = wg_ref.shape[1]
    for c in range(0, ffn_dim, FFN_COL_BLOCK):
        g = _bdot(hn, wg_ref[:, c:c + FFN_COL_BLOCK])
        u = _bdot(hn, wu_ref[:, c:c + FFN_COL_BLOCK])
        a_scr[:, c:c + FFN_COL_BLOCK] = (_silu(g) * u).astype(BF16)
    h = h + _bdot(a_scr[...], wd_ref[...])
    e = _rms(_bdot(p_ref[...].astype(BF16), wp_ref[...]), pn_ref[...], RMS_EPS)
    gate = _sigmoid(_bdot(_rms(h, gn_ref[...], RMS_EPS).astype(BF16), wpg_ref[...]))
    h = h + gate * e
    if final:
        h = _rms(h, fn_ref[...], RMS_EPS)
    o_ref[...] = h


def _ffn_ple_layer(h, p, norm_ffn, w_gate, w_up, w_down, w_proj, ple_norm, gate_norm,
                   w_pgate, final_norm, *, final, tm):
    t, d = h.shape
    ffn_dim = w_gate.shape[1]
    ple_dim = p.shape[1]
    assert t % tm == 0 and ffn_dim % FFN_COL_BLOCK == 0
    row = lambda v: v.reshape(1, d).astype(F32)
    tile = lambda width: pl.BlockSpec((tm, width), lambda i: (i, 0))
    return pl.pallas_call(
        functools.partial(_ffn_ple_kernel, final=final),
        grid=(t // tm,),
        in_specs=[
            tile(d), tile(ple_dim),
            _weight_spec((1, d), 1),
            _weight_spec((d, ffn_dim), 1), _weight_spec((d, ffn_dim), 1),
            _weight_spec((ffn_dim, d), 1),
            _weight_spec((ple_dim, d), 1), _weight_spec((1, d), 1),
            _weight_spec((1, d), 1), _weight_spec((d, d), 1),
            _weight_spec((1, d), 1),
        ],
        out_specs=tile(d),
        out_shape=jax.ShapeDtypeStruct((t, d), F32),
        scratch_shapes=[pltpu.VMEM((tm, ffn_dim), BF16)],
        compiler_params=pltpu.CompilerParams(
            dimension_semantics=("arbitrary",), vmem_limit_bytes=V7X_VMEM_LIMIT_BYTES),
        name="ffn_ple",
    )(h, p, row(norm_ffn), w_gate.astype(BF16), w_up.astype(BF16), w_down.astype(BF16),
      w_proj.astype(BF16), row(ple_norm), row(gate_norm), w_pgate.astype(BF16), row(final_norm))


GMLP_COL_BLOCK = 256


def _gmlp_kernel(h_ref, nm_ref, win_ref, bin_ref, lnw_ref, lnb_ref, ws_ref, bs_ref, wout_ref,
                 o_ref, u_scr, v_scr, y_scr):
    tm = h_ref.shape[0]
    inner = u_scr.shape[1]
    n_chunks = tm // GMLP_CHUNK
    n_groups = inner // GMLP_GROUP_DIM
    h = h_ref[...]
    hn = _rms(h, nm_ref[...], RMS_EPS).astype(BF16)
    for c in range(0, inner, GMLP_COL_BLOCK):
        cols = slice(c, c + GMLP_COL_BLOCK)
        u_scr[:, cols] = _gelu(_bdot(hn, win_ref[:, cols]) + bin_ref[:, cols])
    for c in range(0, inner, GMLP_COL_BLOCK):
        cols = slice(inner + c, inner + c + GMLP_COL_BLOCK)
        v_scr[:, c:c + GMLP_COL_BLOCK] = _gelu(_bdot(hn, win_ref[:, cols]) + bin_ref[:, cols])
    v = v_scr[...]
    vc = v - jnp.mean(v, axis=-1, keepdims=True)
    vn = vc * lax.rsqrt(jnp.mean(vc * vc, axis=-1, keepdims=True) + LN_EPS)
    v_scr[...] = vn * lnw_ref[...] + lnb_ref[...]
    causal = (lax.broadcasted_iota(jnp.int32, (GMLP_CHUNK, GMLP_CHUNK), 0)
              >= lax.broadcasted_iota(jnp.int32, (GMLP_CHUNK, GMLP_CHUNK), 1))
    for g in range(n_groups):
        cols = slice(g * GMLP_GROUP_DIM, (g + 1) * GMLP_GROUP_DIM)
        w = jnp.where(causal, ws_ref[g], 0.0).astype(BF16)
        rhs = jnp.concatenate(
            [v_scr[q * GMLP_CHUNK:(q + 1) * GMLP_CHUNK, cols].astype(BF16) for q in range(n_chunks)],
            axis=1)
        mixed = _bdot(w, rhs) + bs_ref[g]
        for q in range(n_chunks):
            rows = slice(q * GMLP_CHUNK, (q + 1) * GMLP_CHUNK)
            m = mixed[:, q * GMLP_GROUP_DIM:(q + 1) * GMLP_GROUP_DIM]
            y_scr[rows, cols] = (u_scr[rows, cols] * m).astype(BF16)
    o_ref[...] = h + _bdot(y_scr[...], wout_ref[...])


def _gmlp_layer(h, norm_mix, w_in, b_in, ln_w, ln_b, w_s, b_s, w_out, *, tm):
    t, d = h.shape
    inner = w_out.shape[0]
    n_groups = w_s.shape[0]
    assert t % tm == 0 and tm % GMLP_CHUNK == 0 and inner % GMLP_COL_BLOCK == 0
    assert n_groups * GMLP_GROUP_DIM == inner and w_s.shape[1:] == (GMLP_CHUNK, GMLP_CHUNK)
    tile = pl.BlockSpec((tm, d), lambda i: (i, 0))
    return pl.pallas_call(
        _gmlp_kernel,
        grid=(t // tm,),
        in_specs=[
            tile,
            _weight_spec((1, d), 1),
            _weight_spec((d, 2 * inner), 1), _weight_spec((1, 2 * inner), 1),
            _weight_spec((1, inner), 1), _weight_spec((1, inner), 1),
            _weight_spec((n_groups, GMLP_CHUNK, GMLP_CHUNK), 1),
            _weight_spec((n_groups, GMLP_CHUNK, 1), 1),
            _weight_spec((inner, d), 1),
        ],
        out_specs=tile,
        out_shape=jax.ShapeDtypeStruct((t, d), F32),
        scratch_shapes=[pltpu.VMEM((tm, inner), F32), pltpu.VMEM((tm, inner), F32),
                        pltpu.VMEM((tm, inner), BF16)],
        compiler_params=pltpu.CompilerParams(
            dimension_semantics=("arbitrary",), vmem_limit_bytes=V7X_VMEM_LIMIT_BYTES),
        name="gmlp_mixer",
    )(h, norm_mix.reshape(1, d), w_in.astype(BF16), b_in.reshape(1, 2 * inner),
      ln_w.reshape(1, inner), ln_b.reshape(1, inner), w_s, b_s.reshape(n_groups, GMLP_CHUNK, 1),
      w_out.astype(BF16))


SSD_COL_BLOCK = 256


def _ssd_kernel(h_ref, nm_ref, wz_ref, wxbc_ref, wdt3_ref, wdtt_ref, cw_ref, cb_ref,
                dtb3_ref, alog3_ref, dtbc_ref, alogc_ref, dskip_ref, normw_ref,
                e64_ref, e128_ref, wout_ref,
                o_ref,
                ext_scr, zs_scr, xs_scr, b_scr, c_scr, a_scr, mx_scr, csrow_scr, state_scr, yn_scr):
    tt = h_ref.shape[0]
    n_chunks = tt // SSD_CHUNK
    inner = SSD_GROUPS * GROUP_WIDTH
    bc_width = SSD_GROUPS * SSD_STATE
    step = pl.program_id(1)

    @pl.when(step == 0)
    def _():
        ext_scr[0:CONV_HALO, :] = jnp.zeros((CONV_HALO, ext_scr.shape[1]), F32)
        state_scr[...] = jnp.zeros(state_scr.shape, F32)

    h = h_ref[...]
    hn = _rms(h, nm_ref[...], RMS_EPS).astype(BF16)

    for g in range(SSD_GROUPS):
        cols = slice(g * GROUP_WIDTH, (g + 1) * GROUP_WIDTH)
        zs_scr[g] = _silu(_bdot(hn, wz_ref[:, cols]))

    for c in range(0, inner + 2 * bc_width, SSD_COL_BLOCK):
        cols = slice(c, c + SSD_COL_BLOCK)
        ext_scr[CONV_HALO:CONV_HALO + tt, cols] = _bdot(hn, wxbc_ref[:, cols])

    def conv(cols):
        acc = cb_ref[:, cols]
        for k in range(CONV_K):
            start = CONV_HALO - (CONV_K - 1) + k
            acc = acc + cw_ref[k:k + 1, cols] * ext_scr[start:start + tt, cols]
        return _silu(acc)

    for g in range(SSD_GROUPS):
        xs_scr[g] = conv(slice(g * GROUP_WIDTH, (g + 1) * GROUP_WIDTH))
        b_scr[g] = conv(slice(inner + g * SSD_STATE, inner + (g + 1) * SSD_STATE)).astype(BF16)
        c_scr[g] = conv(slice(inner + bc_width + g * SSD_STATE,
                              inner + bc_width + (g + 1) * SSD_STATE)).astype(BF16)
    ext_scr[0:CONV_HALO, :] = ext_scr[tt:tt + CONV_HALO, :]

    a3 = -jnp.exp(alog3_ref[...])
    dt3 = _softplus(_bdot(hn, wdt3_ref[...]) + dtb3_ref[...])
    da3 = dt3 * a3
    dt_rows = lax.dot_general(wdtt_ref[...], hn, (((1,), (1,)), ((), ())),
                              preferred_element_type=F32)

    row_i = lax.broadcasted_iota(jnp.int32, (SSD_CHUNK, SSD_CHUNK), 0)
    col_i = lax.broadcasted_iota(jnp.int32, (SSD_CHUNK, SSD_CHUNK), 1)
    causal = row_i >= col_i
    lower = jnp.where(causal, 1.0, 0.0).astype(BF16)
    upper = jnp.where(row_i <= col_i, 1.0, 0.0).astype(BF16)

    for q in range(n_chunks):
        rows = slice(q * SSD_CHUNK, (q + 1) * SSD_CHUNK)
        hi, mid, lo = _bf16_pieces(da3[rows])
        cs = _bdot(lower, hi) + _bdot(lower, mid) + _bdot(lower, lo)
        total = cs[SSD_CHUNK - 1:SSD_CHUNK, :]
        a_scr[q] = _split_select(cs)
        mx_scr[q, 0:SSD_CHUNK] = _split_select(dt3[rows])
        mx_scr[q, SSD_CHUNK:2 * SSD_CHUNK] = _split_select(jnp.exp(total - cs))
        mx_scr[q, 2 * SSD_CHUNK:3 * SSD_CHUNK] = _split_select(jnp.exp(cs))
        dt_r = _softplus(dt_rows[:, rows] + dtbc_ref[...])
        da_r = dt_r * (-jnp.exp(alogc_ref[...]))
        hi, mid, lo = _bf16_pieces(da_r)
        csrow_scr[q] = _bdot(hi, upper) + _bdot(mid, upper) + _bdot(lo, upper)

    lane_head = lax.broadcasted_iota(jnp.int32, (SSD_CHUNK, GROUP_WIDTH), 1) // SSD_HEADDIM

    def group_body(g, carry):
        e64 = e64_ref[g]
        e128 = e128_ref[g]
        for q in range(n_chunks):
            rows = slice(q * SSD_CHUNK, (q + 1) * SSD_CHUNK)
            ccx = _bdot(a_scr[q], e128)
            ex = _bdot(mx_scr[q], e64)
            dt_x = ex[0:SSD_CHUNK]
            dec_state_x = ex[SSD_CHUNK:2 * SSD_CHUNK]
            dec_out_x = ex[2 * SSD_CHUNK:3 * SSD_CHUNK]
            xg = xs_scr[g, rows, :]
            bg = b_scr[g, rows, :]
            cg = c_scr[g, rows, :]
            xr = xg * dt_x
            xr_b = xr.astype(BF16)
            xd_b = (xr * dec_state_x).astype(BF16)
            cb = lax.dot_general(cg, bg, (((1,), (1,)), ((), ())), preferred_element_type=F32)
            w_parts, x_parts = [], []
            for r in range(HEADS_PER_GROUP):
                crow = csrow_scr[q, pl.ds(g * HEADS_PER_GROUP + r, 1), :]
                diff = ccx[:, r * SSD_CHUNK:(r + 1) * SSD_CHUNK] - crow
                decay = jnp.exp(jnp.where(causal, diff, -jnp.inf))
                w_parts.append((cb * decay).astype(BF16))
                x_parts.append(jnp.where(lane_head == r, xr_b, jnp.zeros_like(xr_b)))
            y_diag = _bdot(jnp.concatenate(w_parts, axis=1), jnp.concatenate(x_parts, axis=0))
            prev = state_scr[g]
            y_off = _bdot(cg, prev.astype(BF16)) * dec_out_x
            new_states = lax.dot_general(bg, xd_b, (((0,), (0,)), ((), ())),
                                         preferred_element_type=F32)
            state_scr[g] = prev * dec_out_x[SSD_CHUNK - 1:SSD_CHUNK, :] + new_states
            y = y_diag + y_off + xg * dskip_ref[g]
            gz = y * zs_scr[g, rows, :]
            gn = gz * lax.rsqrt(jnp.mean(gz * gz, axis=-1, keepdims=True) + LN_EPS)
            yn_scr[g, rows, :] = (gn * normw_ref[g]).astype(BF16)
        return carry

    lax.fori_loop(0, SSD_GROUPS, group_body, 0)

    yn = jnp.concatenate([yn_scr[g] for g in range(SSD_GROUPS)], axis=1)
    o_ref[...] = h + _bdot(yn, wout_ref[...])


def _expansion_matrices():
    k = np.arange(SPLIT_WIDTH)[None, :, None] % SSD_HEADS
    g = np.arange(SSD_GROUPS)[:, None, None]
    head64 = g * HEADS_PER_GROUP + np.arange(GROUP_WIDTH)[None, None, :] // SSD_HEADDIM
    head128 = g * HEADS_PER_GROUP + np.arange(HEADS_PER_GROUP * SSD_CHUNK)[None, None, :] // SSD_CHUNK
    return (jnp.asarray(k == head64, dtype=BF16), jnp.asarray(k == head128, dtype=BF16))


def _ssd_layer(h, norm_mix, w_in, conv_w, conv_b, dt_bias, a_log, d_skip, norm_w, w_out, *, tt):
    b, s, d = h.shape
    inner = SSD_GROUPS * GROUP_WIDTH
    conv_dim = inner + 2 * SSD_GROUPS * SSD_STATE
    assert s % tt == 0 and tt % SSD_CHUNK == 0 and tt >= CONV_HALO
    assert w_in.shape == (d, 2 * inner + 2 * SSD_GROUPS * SSD_STATE + SSD_HEADS)
    assert conv_w.shape == (CONV_K, conv_dim) and w_out.shape == (inner, d)
    n_chunks = tt // SSD_CHUNK

    w_z = w_in[:, :inner].astype(BF16)
    w_xbc = w_in[:, inner:inner + conv_dim].astype(BF16)
    w_dt = w_in[:, inner + conv_dim:].astype(BF16)
    w_dt3 = jnp.tile(w_dt, (1, 3))
    w_dtt = w_dt.T
    tile3 = lambda v: jnp.tile(v.reshape(1, SSD_HEADS).astype(F32), (1, 3))
    column = lambda v: jnp.broadcast_to(v.reshape(SSD_HEADS, 1).astype(F32), (SSD_HEADS, SSD_CHUNK))
    dskip_x = jnp.repeat(d_skip.astype(F32), SSD_HEADDIM).reshape(SSD_GROUPS, 1, GROUP_WIDTH)
    normw_g = norm_w.astype(F32).reshape(SSD_GROUPS, 1, GROUP_WIDTH)
    e64, e128 = _expansion_matrices()

    tile = pl.BlockSpec((None, tt, d), lambda bi, i: (bi, i, 0))
    ws = lambda shape: _weight_spec(shape, 2)
    return pl.pallas_call(
        _ssd_kernel,
        grid=(b, s // tt),
        in_specs=[
            tile,
            ws((1, d)),
            ws((d, inner)), ws((d, conv_dim)), ws((d, SPLIT_WIDTH)), ws((SSD_HEADS, d)),
            ws((CONV_K, conv_dim)), ws((1, conv_dim)),
            ws((1, SPLIT_WIDTH)), ws((1, SPLIT_WIDTH)),
            ws((SSD_HEADS, SSD_CHUNK)), ws((SSD_HEADS, SSD_CHUNK)),
            ws((SSD_GROUPS, 1, GROUP_WIDTH)), ws((SSD_GROUPS, 1, GROUP_WIDTH)),
            ws((SSD_GROUPS, SPLIT_WIDTH, GROUP_WIDTH)),
            ws((SSD_GROUPS, SPLIT_WIDTH, HEADS_PER_GROUP * SSD_CHUNK)),
            ws((inner, d)),
        ],
        out_specs=tile,
        out_shape=jax.ShapeDtypeStruct((b, s, d), F32),
        scratch_shapes=[
            pltpu.VMEM((tt + CONV_HALO, conv_dim), F32),
            pltpu.VMEM((SSD_GROUPS, tt, GROUP_WIDTH), F32),
            pltpu.VMEM((SSD_GROUPS, tt, GROUP_WIDTH), F32),
            pltpu.VMEM((SSD_GROUPS, tt, SSD_STATE), BF16),
            pltpu.VMEM((SSD_GROUPS, tt, SSD_STATE), BF16),
            pltpu.VMEM((n_chunks, SSD_CHUNK, SPLIT_WIDTH), BF16),
            pltpu.VMEM((n_chunks, 3 * SSD_CHUNK, SPLIT_WIDTH), BF16),
            pltpu.VMEM((n_chunks, SSD_HEADS, SSD_CHUNK), F32),
            pltpu.VMEM((SSD_GROUPS, SSD_STATE, GROUP_WIDTH), F32),
            pltpu.VMEM((SSD_GROUPS, tt, GROUP_WIDTH), BF16),
        ],
        compiler_params=pltpu.CompilerParams(
            dimension_semantics=("arbitrary", "arbitrary"),
            vmem_limit_bytes=V7X_VMEM_LIMIT_BYTES),
        name="ssd_mixer",
    )(h, norm_mix.reshape(1, d), w_z, w_xbc, w_dt3, w_dtt, conv_w, conv_b.reshape(1, conv_dim),
      tile3(dt_bias), tile3(a_log), column(dt_bias), column(a_log), dskip_x, normw_g,
      e64, e128, w_out.astype(BF16))


def _pick_tile(n, preferred):
    tile = min(n, preferred)
    assert n % tile == 0
    return tile


def kernel(x, p, norm_mix, norm_ffn, ssd_w_in, ssd_conv_w, ssd_conv_b, ssd_dt_bias, ssd_a_log, ssd_d, ssd_norm_w, ssd_w_out, gmlp_w_in, gmlp_b_in, gmlp_ln_w, gmlp_ln_b, gmlp_w_s, gmlp_b_s, gmlp_w_out, ffn_w_gate, ffn_w_up, ffn_w_down, ple_w_proj, ple_norm, ple_gate_norm, ple_w_gate, final_norm):
    b, s, d = x.shape
    t = b * s
    depth = norm_mix.shape[0]
    h = x
    for i in range(depth):
        j = i // N_MIXERS
        if i % N_MIXERS == 0:
            h = _ssd_layer(h.reshape(b, s, d), norm_mix[i], ssd_w_in[j], ssd_conv_w[j], ssd_conv_b[j],
                           ssd_dt_bias[j], ssd_a_log[j], ssd_d[j], ssd_norm_w[j], ssd_w_out[j],
                           tt=_pick_tile(s, 256))
        else:
            h = _gmlp_layer(h.reshape(t, d), norm_mix[i], gmlp_w_in[j], gmlp_b_in[j], gmlp_ln_w[j],
                            gmlp_ln_b[j], gmlp_w_s[j], gmlp_b_s[j], gmlp_w_out[j],
                            tm=_pick_tile(t, 512))
        h = _ffn_ple_layer(h.reshape(t, d), p[i].reshape(t, -1), norm_ffn[i], ffn_w_gate[i], ffn_w_up[i],
                           ffn_w_down[i], ple_w_proj[i], ple_norm[i], ple_gate_norm[i], ple_w_gate[i],
                           final_norm, final=(i == depth - 1), tm=_pick_tile(t, 512))
    return h.reshape(b, s, d)
```

```python
import functools
import math

import jax
import jax.numpy as jnp
from jax import lax
from jax.experimental import pallas as pl
from jax.experimental.pallas import tpu as pltpu

F32 = jnp.float32
BF16 = jnp.bfloat16

DEPTH = 4
N_MIXERS = 2

SSD_HEADDIM = 64
SSD_GROUPS = 8
SSD_STATE = 128
SSD_CHUNK = 128
CONV_K = 4
HEADS_PER_GROUP = 4
GROUP_WIDTH = HEADS_PER_GROUP * SSD_HEADDIM
SSD_HEADS = SSD_GROUPS * HEADS_PER_GROUP
SPLIT_WIDTH = 3 * SSD_HEADS

GMLP_CHUNK = 128
GMLP_GROUP_DIM = 128

RMS_EPS = 1e-6
LN_EPS = 1e-5

V7X_VMEM_LIMIT_BYTES = 56 * 1024 * 1024


def _bdot(a, b):
    return jnp.dot(a, b, preferred_element_type=F32)


def _rms(x, w, eps):
    return x * lax.rsqrt(jnp.mean(x * x, axis=-1, keepdims=True) + eps) * w


def _sigmoid(x):
    return 0.5 + 0.5 * jnp.tanh(0.5 * x)


def _silu(x):
    hx = 0.5 * x
    return hx + hx * jnp.tanh(hx)


def _gelu(x):
    return 0.5 * x * (1.0 + lax.erf(x * (1.0 / math.sqrt(2.0))))


def _softplus(x):
    return jnp.maximum(x, 0.0) + jnp.log1p(jnp.exp(-jnp.abs(x)))


def _bf16_pieces(x):
    hi = x.astype(BF16)
    r1 = x - hi.astype(F32)
    mid = r1.astype(BF16)
    lo = (r1 - mid.astype(F32)).astype(BF16)
    return hi, mid, lo


def _split_select(x3):
    hi, mid, lo = _bf16_pieces(x3)
    lane = lax.broadcasted_iota(jnp.int32, x3.shape, 1)
    return jnp.where(lane < SSD_HEADS, hi, jnp.where(lane < 2 * SSD_HEADS, mid, lo))


def _weight_spec(shape, grid_rank):
    zeros = (0,) * len(shape)
    if grid_rank == 1:
        index_map = lambda i: zeros
    else:
        index_map = lambda b, i: zeros
    return pl.BlockSpec(shape, index_map, pipeline_mode=pl.Buffered(1))


FFN_COL_BLOCK = 256


def _ffn_ple_kernel(h_ref, p_ref, nf_ref, wg_ref, wu_ref, wd_ref, wp_ref, pn_ref,
                    gn_ref, wpg_ref, fn_ref, o_ref, hn_scr, a_scr, *, final):
    h = h_ref[...]
    hn_scr[...] = _rms(h, nf_ref[...], RMS_EPS).astype(BF16)
    hn = hn_scr[...]
    ffn_dim = wg_ref.shape[1]
    for c in range(0, ffn_dim, FFN_COL_BLOCK):
        g = _bdot(hn, wg_ref[:, c:c + FFN_COL_BLOCK])
        u = _bdot(hn, wu_ref[:, c:c + FFN_COL_BLOCK])
        a_scr[:, c:c + FFN_COL_BLOCK] = (_silu(g) * u).astype(BF16)
    h = h + _bdot(a_scr[...], wd_ref[...])
    e = _rms(_bdot(p_ref[...].astype(BF16), wp_ref[...]), pn_ref[...], RMS_EPS)
    gate = _sigmoid(_bdot(_rms(h, gn_ref[...], RMS_EPS).astype(BF16), wpg_ref[...]))
    h = h + gate * e
    if final:
        h = _rms(h, fn_ref[...], RMS_EPS)
    o_ref[...] = h


def _ffn_ple_layer(h, p, norm_ffn, w_gate, w_up, w_down, w_proj, ple_norm, gate_norm,
                   w_pgate, final_norm, *, final, tm):
    t, d = h.shape
    ffn_dim = w_gate.shape[1]
    ple_dim = p.shape[1]
    assert t % tm == 0 and ffn_dim % FFN_COL_BLOCK == 0
    row = lambda v: v.reshape(1, d).astype(F32)
    tile = lambda width: pl.BlockSpec((tm, width), lambda i: (i, 0))
    return pl.pallas_call(
        functools.partial(_ffn_ple_kernel, final=final),
        grid=(t // tm,),
        in_specs=[
            tile(d), tile(ple_dim),
            _weight_spec((1, d), 1),
            _weight_spec((d, ffn_dim), 1), _weight_spec((d, ffn_dim), 1),
            _weight_spec((ffn_dim, d), 1),
            _weight_spec((ple_dim, d), 1), _weight_spec((1, d), 1),
            _weight_spec((1, d), 1), _weight_spec((d, d), 1),
            _weight_spec((1, d), 1),
        ],
        out_specs=tile(d),
        out_shape=jax.ShapeDtypeStruct((t, d), F32),
        scratch_shapes=[pltpu.VMEM((tm, d), BF16), pltpu.VMEM((tm, ffn_dim), BF16)],
        compiler_params=pltpu.CompilerParams(
            dimension_semantics=("arbitrary",), vmem_limit_bytes=V7X_VMEM_LIMIT_BYTES),
        name="ffn_ple",
    )(h, p, row(norm_ffn), w_gate.astype(BF16), w_up.astype(BF16), w_down.astype(BF16),
      w_proj.astype(BF16), row(ple_norm), row(gate_norm), w_pgate.astype(BF16), row(final_norm))


GMLP_COL_BLOCK = 256


def _gmlp_kernel(h_ref, nm_ref, win_ref, bin_ref, lnw_ref, lnb_ref, ws_ref, bs_ref, wout_ref,
                 o_ref, hn_scr, u_scr, v_scr, y_scr):
    tm = h_ref.shape[0]
    inner = u_scr.shape[1]
    n_chunks = tm // GMLP_CHUNK
    n_groups = inner // GMLP_GROUP_DIM
    hn_scr[...] = _rms(h_ref[...], nm_ref[...], RMS_EPS).astype(BF16)
    hn = hn_scr[...]
    for c in range(0, inner, GMLP_COL_BLOCK):
        cols = slice(inner + c, inner + c + GMLP_COL_BLOCK)
        v_scr[:, c:c + GMLP_COL_BLOCK] = _gelu(_bdot(hn, win_ref[:, cols]) + bin_ref[:, cols])
    v = v_scr[...]
    vc = v - jnp.mean(v, axis=-1, keepdims=True)
    vn = vc * lax.rsqrt(jnp.mean(vc * vc, axis=-1, keepdims=True) + LN_EPS)
    v_scr[...] = vn * lnw_ref[...] + lnb_ref[...]
    for c in range(0, inner, GMLP_COL_BLOCK):
        cols = slice(c, c + GMLP_COL_BLOCK)
        u_scr[:, cols] = _gelu(_bdot(hn, win_ref[:, cols]) + bin_ref[:, cols])
    causal = (lax.broadcasted_iota(jnp.int32, (GMLP_CHUNK, GMLP_CHUNK), 0)
              >= lax.broadcasted_iota(jnp.int32, (GMLP_CHUNK, GMLP_CHUNK), 1))
    for g in range(n_groups):
        cols = slice(g * GMLP_GROUP_DIM, (g + 1) * GMLP_GROUP_DIM)
        w = jnp.where(causal, ws_ref[g], 0.0).astype(BF16)
        rhs = jnp.concatenate(
            [v_scr[q * GMLP_CHUNK:(q + 1) * GMLP_CHUNK, cols].astype(BF16) for q in range(n_chunks)],
            axis=1)
        mixed = _bdot(w, rhs) + bs_ref[g]
        for q in range(n_chunks):
            rows = slice(q * GMLP_CHUNK, (q + 1) * GMLP_CHUNK)
            m = mixed[:, q * GMLP_GROUP_DIM:(q + 1) * GMLP_GROUP_DIM]
            y_scr[rows, cols] = (u_scr[rows, cols] * m).astype(BF16)
    o_ref[...] = h_ref[...] + _bdot(y_scr[...], wout_ref[...])


def _gmlp_layer(h, norm_mix, w_in, b_in, ln_w, ln_b, w_s, b_s, w_out, *, tm):
    t, d = h.shape
    inner = w_out.shape[0]
    n_groups = w_s.shape[0]
    assert t % tm == 0 and tm % GMLP_CHUNK == 0 and inner % GMLP_COL_BLOCK == 0
    assert n_groups * GMLP_GROUP_DIM == inner and w_s.shape[1:] == (GMLP_CHUNK, GMLP_CHUNK)
    tile = pl.BlockSpec((tm, d), lambda i: (i, 0))
    return pl.pallas_call(
        _gmlp_kernel,
        grid=(t // tm,),
        in_specs=[
            tile,
            _weight_spec((1, d), 1),
            _weight_spec((d, 2 * inner), 1), _weight_spec((1, 2 * inner), 1),
            _weight_spec((1, inner), 1), _weight_spec((1, inner), 1),
            _weight_spec((n_groups, GMLP_CHUNK, GMLP_CHUNK), 1),
            _weight_spec((n_groups, GMLP_CHUNK, 1), 1),
            _weight_spec((inner, d), 1),
        ],
        out_specs=tile,
        out_shape=jax.ShapeDtypeStruct((t, d), F32),
        scratch_shapes=[pltpu.VMEM((tm, d), BF16), pltpu.VMEM((tm, inner), F32),
                        pltpu.VMEM((tm, inner), F32), pltpu.VMEM((tm, inner), BF16)],
        compiler_params=pltpu.CompilerParams(
            dimension_semantics=("arbitrary",), vmem_limit_bytes=V7X_VMEM_LIMIT_BYTES),
        name="gmlp_mixer",
    )(h, norm_mix.reshape(1, d), w_in.astype(BF16), b_in.reshape(1, 2 * inner),
      ln_w.reshape(1, inner), ln_b.reshape(1, inner), w_s, b_s.reshape(n_groups, GMLP_CHUNK, 1),
      w_out.astype(BF16))


XBC_GROUP_WIDTH = GROUP_WIDTH + 2 * SSD_STATE
SUBLANES = 8
LANES = 128
SEGMENT = SSD_CHUNK // SUBLANES
HALO_ROWS = (CONV_K - 1) * SUBLANES


def _ssd_kernel(h_ref, nm_ref, wz_ref, wxbc_ref, wdt_ref, wdtt_ref, cw_ref, cb_ref,
                dtb_ref, alog_ref, dtbc_ref, alogc_ref, dskip_ref, normw_ref, e64_ref, wout_ref,
                o_ref,
                stage_scr, hp_scr, hn_scr, halo_scr, zs_scr, xs_scr, b_scr, c_scr, cs_scr, mx_scr,
                csrow_scr, state_scr, yn_scr):
    tt = h_ref.shape[0]
    n_chunks = tt // SSD_CHUNK
    step = pl.program_id(1)

    @pl.when(step == 0)
    def _():
        halo_scr[...] = jnp.zeros(halo_scr.shape, F32)
        state_scr[...] = jnp.zeros(state_scr.shape, F32)

    n_lane_tiles = h_ref.shape[1] // LANES
    for c in range(n_lane_tiles):
        stage_scr[c] = h_ref[:, c * LANES:(c + 1) * LANES]
    for q in range(n_chunks):
        for i in range(SEGMENT):
            row = q * SSD_CHUNK + SUBLANES * i
            for c in range(n_lane_tiles):
                hp_scr[row:row + SUBLANES, c * LANES:(c + 1) * LANES] = (
                    stage_scr[c, pl.ds(q * SSD_CHUNK + i, SUBLANES, stride=SEGMENT), :])
    hn_scr[...] = _rms(hp_scr[...], nm_ref[...], RMS_EPS).astype(BF16)
    hn = hn_scr[...]

    dt = _softplus(_bdot(hn, wdt_ref[...]) + dtb_ref[...])
    da = dt * (-jnp.exp(alog_ref[...]))
    dt_rows = lax.dot_general(wdtt_ref[...], hn, (((1,), (1,)), ((), ())),
                              preferred_element_type=F32)

    def position(index):
        return (index % SUBLANES) * SEGMENT + index // SUBLANES

    pos_row = position(lax.broadcasted_iota(jnp.int32, (SSD_CHUNK, SSD_CHUNK), 0))
    pos_col = position(lax.broadcasted_iota(jnp.int32, (SSD_CHUNK, SSD_CHUNK), 1))
    causal = pos_row >= pos_col
    lower = jnp.where(causal, 1.0, 0.0).astype(BF16)
    upper = jnp.where(pos_row <= pos_col, 1.0, 0.0).astype(BF16)
    last_sublane = lax.broadcasted_iota(
        jnp.int32, (HALO_ROWS, XBC_GROUP_WIDTH), 0) % SUBLANES == SUBLANES - 1

    for q in range(n_chunks):
        rows = slice(q * SSD_CHUNK, (q + 1) * SSD_CHUNK)
        hi, mid, lo = _bf16_pieces(da[rows])
        cs = _bdot(lower, hi) + _bdot(lower, mid) + _bdot(lower, lo)
        total = cs[SSD_CHUNK - 1:SSD_CHUNK, :]
        cs_scr[q] = cs
        mx_scr[q, 0:SSD_CHUNK] = _split_select(dt[rows])
        mx_scr[q, SSD_CHUNK:2 * SSD_CHUNK] = _split_select(jnp.exp(total - cs))
        mx_scr[q, 2 * SSD_CHUNK:3 * SSD_CHUNK] = _split_select(jnp.exp(cs))
        dt_r = _softplus(dt_rows[:, rows] + dtbc_ref[...])
        da_r = dt_r * (-jnp.exp(alogc_ref[...]))
        hi, mid, lo = _bf16_pieces(da_r)
        csrow_scr[q] = _bdot(hi, upper) + _bdot(mid, upper) + _bdot(lo, upper)

    lane_head = lax.broadcasted_iota(jnp.int32, (SSD_CHUNK, GROUP_WIDTH), 1) // SSD_HEADDIM

    for g in range(SSD_GROUPS):
        cols = slice(g * XBC_GROUP_WIDTH, (g + 1) * XBC_GROUP_WIDTH)
        raw = _bdot(hn, wxbc_ref[:, cols])
        for q in range(n_chunks):
            rows = slice(q * SSD_CHUNK, (q + 1) * SSD_CHUNK)
            cur = raw[rows]
            tail = halo_scr[:, cols] if q == 0 else raw[q * SSD_CHUNK - HALO_ROWS:q * SSD_CHUNK]
            wrapped = jnp.where(last_sublane, tail, cur[SSD_CHUNK - HALO_ROWS:])
            boundary = [pltpu.roll(wrapped[k * SUBLANES:(k + 1) * SUBLANES], 1, axis=0)
                        for k in range(CONV_K - 1)]
            window = jnp.concatenate(boundary + [cur], axis=0)
            acc = cb_ref[:, cols] + cw_ref[CONV_K - 1:CONV_K, cols] * cur
            for k in range(CONV_K - 1):
                acc = acc + cw_ref[k:k + 1, cols] * window[k * SUBLANES:k * SUBLANES + SSD_CHUNK]
            xbc = _silu(acc)
            xs_scr[g, rows, :] = xbc[:, :GROUP_WIDTH]
            b_scr[g, rows, :] = xbc[:, GROUP_WIDTH:GROUP_WIDTH + SSD_STATE].astype(BF16)
            c_scr[g, rows, :] = xbc[:, GROUP_WIDTH + SSD_STATE:].astype(BF16)
        halo_scr[:, cols] = raw[tt - HALO_ROWS:]
        zs_scr[g] = _silu(_bdot(hn, wz_ref[:, g * GROUP_WIDTH:(g + 1) * GROUP_WIDTH]))
        e64 = e64_ref[g]
        for q in range(n_chunks):
            rows = slice(q * SSD_CHUNK, (q + 1) * SSD_CHUNK)
            ex = _bdot(mx_scr[q], e64)
            dt_x = ex[0:SSD_CHUNK]
            dec_state_x = ex[SSD_CHUNK:2 * SSD_CHUNK]
            dec_out_x = ex[2 * SSD_CHUNK:3 * SSD_CHUNK]
            xg = xs_scr[g, rows, :]
            bg = b_scr[g, rows, :]
            cg = c_scr[g, rows, :]
            xr = xg * dt_x
            xr_b = xr.astype(BF16)
            xd_b = (xr * dec_state_x).astype(BF16)
            cb = lax.dot_general(cg, bg, (((1,), (1,)), ((), ())), preferred_element_type=F32)
            w_parts, x_parts = [], []
            for r in range(HEADS_PER_GROUP):
                head = g * HEADS_PER_GROUP + r
                crow = csrow_scr[q, head:head + 1, :]
                ccol = jnp.broadcast_to(cs_scr[q, :, head:head + 1], (SSD_CHUNK, SSD_CHUNK))
                diff = ccol - crow
                decay = jnp.exp(jnp.where(causal, diff, -jnp.inf))
                w_parts.append((cb * decay).astype(BF16))
                x_parts.append(jnp.where(lane_head == r, xr_b, jnp.zeros_like(xr_b)))
            y_diag = _bdot(jnp.concatenate(w_parts, axis=1), jnp.concatenate(x_parts, axis=0))
            prev = state_scr[g]
            y_off = _bdot(cg, prev.astype(BF16)) * dec_out_x
            new_states = lax.dot_general(bg, xd_b, (((0,), (0,)), ((), ())),
                                         preferred_element_type=F32)
            state_scr[g] = prev * dec_out_x[SSD_CHUNK - 1:SSD_CHUNK, :] + new_states
            y = y_diag + y_off + xg * dskip_ref[g]
            gz = y * zs_scr[g, rows, :]
            gn = gz * lax.rsqrt(jnp.mean(gz * gz, axis=-1, keepdims=True) + LN_EPS)
            yn_scr[rows, g * GROUP_WIDTH:(g + 1) * GROUP_WIDTH] = (gn * normw_ref[g]).astype(BF16)

    hp_scr[...] = hp_scr[...] + _bdot(yn_scr[...], wout_ref[...])
    for q in range(n_chunks):
        for i in range(SEGMENT):
            row = q * SSD_CHUNK + SUBLANES * i
            for c in range(n_lane_tiles):
                stage_scr[c, pl.ds(q * SSD_CHUNK + i, SUBLANES, stride=SEGMENT), :] = (
                    hp_scr[row:row + SUBLANES, c * LANES:(c + 1) * LANES])
    for c in range(n_lane_tiles):
        o_ref[:, c * LANES:(c + 1) * LANES] = stage_scr[c]


def _ssd_layer(h, norm_mix, w_in, conv_w, conv_b, dt_bias, a_log, d_skip, norm_w, w_out, *, tt):
    b, s, d = h.shape
    inner = SSD_GROUPS * GROUP_WIDTH
    conv_dim = inner + 2 * SSD_GROUPS * SSD_STATE
    assert s % tt == 0 and tt % SSD_CHUNK == 0
    assert w_in.shape == (d, 2 * inner + 2 * SSD_GROUPS * SSD_STATE + SSD_HEADS)
    assert conv_w.shape == (CONV_K, conv_dim) and w_out.shape == (inner, d)
    n_chunks = tt // SSD_CHUNK

    def by_group(v):
        lead = v.shape[0]
        bc = SSD_GROUPS * SSD_STATE
        parts = (v[:, :inner].reshape(lead, SSD_GROUPS, GROUP_WIDTH),
                 v[:, inner:inner + bc].reshape(lead, SSD_GROUPS, SSD_STATE),
                 v[:, inner + bc:].reshape(lead, SSD_GROUPS, SSD_STATE))
        return jnp.concatenate(parts, axis=2).reshape(lead, conv_dim)

    w_z = w_in[:, :inner].astype(BF16)
    w_xbc = by_group(w_in[:, inner:inner + conv_dim].astype(BF16))
    conv_w = by_group(conv_w)
    conv_b = by_group(conv_b.reshape(1, conv_dim))
    w_dt = w_in[:, inner + conv_dim:].astype(BF16)
    w_dtt = w_dt.T
    w_dt = jnp.tile(w_dt, (1, 3))
    as_row = lambda v: jnp.tile(v.reshape(1, SSD_HEADS).astype(F32), (1, 3))
    piece_head = jnp.arange(SPLIT_WIDTH)[None, :, None] % SSD_HEADS
    lane_head = (jnp.arange(SSD_GROUPS)[:, None, None] * HEADS_PER_GROUP
                 + jnp.arange(GROUP_WIDTH)[None, None, :] // SSD_HEADDIM)
    e64 = (piece_head == lane_head).astype(BF16)
    column = lambda v: jnp.broadcast_to(v.reshape(SSD_HEADS, 1).astype(F32), (SSD_HEADS, SSD_CHUNK))
    dskip_x = jnp.repeat(d_skip.astype(F32), SSD_HEADDIM).reshape(SSD_GROUPS, 1, GROUP_WIDTH)
    normw_g = norm_w.astype(F32).reshape(SSD_GROUPS, 1, GROUP_WIDTH)

    tile = pl.BlockSpec((None, tt, d), lambda bi, i: (bi, i, 0))
    ws = lambda shape: _weight_spec(shape, 2)
    return pl.pallas_call(
        _ssd_kernel,
        grid=(b, s // tt),
        in_specs=[
            tile,
            ws((1, d)),
            ws((d, inner)), ws((d, conv_dim)), ws((d, SPLIT_WIDTH)), ws((SSD_HEADS, d)),
            ws((CONV_K, conv_dim)), ws((1, conv_dim)),
            ws((1, SPLIT_WIDTH)), ws((1, SPLIT_WIDTH)),
            ws((SSD_HEADS, SSD_CHUNK)), ws((SSD_HEADS, SSD_CHUNK)),
            ws((SSD_GROUPS, 1, GROUP_WIDTH)), ws((SSD_GROUPS, 1, GROUP_WIDTH)),
            ws((SSD_GROUPS, SPLIT_WIDTH, GROUP_WIDTH)),
            ws((inner, d)),
        ],
        out_specs=tile,
        out_shape=jax.ShapeDtypeStruct((b, s, d), F32),
        scratch_shapes=[
            pltpu.VMEM((d // LANES, tt, LANES), F32),
            pltpu.VMEM((tt, d), F32),
            pltpu.VMEM((tt, d), BF16),
            pltpu.VMEM((HALO_ROWS, conv_dim), F32),
            pltpu.VMEM((SSD_GROUPS, tt, GROUP_WIDTH), F32),
            pltpu.VMEM((SSD_GROUPS, tt, GROUP_WIDTH), F32),
            pltpu.VMEM((SSD_GROUPS, tt, SSD_STATE), BF16),
            pltpu.VMEM((SSD_GROUPS, tt, SSD_STATE), BF16),
            pltpu.VMEM((n_chunks, SSD_CHUNK, SPLIT_WIDTH), F32),
            pltpu.VMEM((n_chunks, 3 * SSD_CHUNK, SPLIT_WIDTH), BF16),
            pltpu.VMEM((n_chunks, SSD_HEADS, SSD_CHUNK), F32),
            pltpu.VMEM((SSD_GROUPS, SSD_STATE, GROUP_WIDTH), F32),
            pltpu.VMEM((tt, inner), BF16),
        ],
        compiler_params=pltpu.CompilerParams(
            dimension_semantics=("arbitrary", "arbitrary"),
            vmem_limit_bytes=V7X_VMEM_LIMIT_BYTES),
        name="ssd_mixer",
    )(h, norm_mix.reshape(1, d), w_z, w_xbc, w_dt, w_dtt, conv_w, conv_b,
      as_row(dt_bias), as_row(a_log), column(dt_bias), column(a_log), dskip_x, normw_g,
      e64, w_out.astype(BF16))


def _pick_tile(n, preferred):
    tile = min(n, preferred)
    assert n % tile == 0
    return tile


def kernel(x, p, norm_mix, norm_ffn, ssd_w_in, ssd_conv_w, ssd_conv_b, ssd_dt_bias, ssd_a_log, ssd_d, ssd_norm_w, ssd_w_out, gmlp_w_in, gmlp_b_in, gmlp_ln_w, gmlp_ln_b, gmlp_w_s, gmlp_b_s, gmlp_w_out, ffn_w_gate, ffn_w_up, ffn_w_down, ple_w_proj, ple_norm, ple_gate_norm, ple_w_gate, final_norm):
    b, s, d = x.shape
    t = b * s
    depth = norm_mix.shape[0]
    h = x
    for i in range(depth):
        j = i // N_MIXERS
        if i % N_MIXERS == 0:
            h = _ssd_layer(h.reshape(b, s, d), norm_mix[i], ssd_w_in[j], ssd_conv_w[j], ssd_conv_b[j],
                           ssd_dt_bias[j], ssd_a_log[j], ssd_d[j], ssd_norm_w[j], ssd_w_out[j],
                           tt=_pick_tile(s, 256))
        else:
            h = _gmlp_layer(h.reshape(t, d), norm_mix[i], gmlp_w_in[j], gmlp_b_in[j], gmlp_ln_w[j],
                            gmlp_ln_b[j], gmlp_w_s[j], gmlp_b_s[j], gmlp_w_out[j],
                            tm=_pick_tile(t, 512))
        h = _ffn_ple_layer(h.reshape(t, d), p[i].reshape(t, -1), norm_ffn[i], ffn_w_gate[i], ffn_w_up[i],
                           ffn_w_down[i], ple_w_proj[i], ple_norm[i], ple_gate_norm[i], ple_w_gate[i],
                           final_norm, final=(i == depth - 1), tm=_pick_tile(t, 512))
    return h.reshape(b, s, d)
```

```python
import functools
import math

import jax
import jax.numpy as jnp
from jax import lax
from jax.experimental import pallas as pl
from jax.experimental.pallas import tpu as pltpu

F32 = jnp.float32
BF16 = jnp.bfloat16

DEPTH = 4
N_MIXERS = 2

SSD_HEADDIM = 64
SSD_GROUPS = 8
SSD_STATE = 128
SSD_CHUNK = 128
CONV_K = 4
HEADS_PER_GROUP = 4
GROUP_WIDTH = HEADS_PER_GROUP * SSD_HEADDIM
SSD_HEADS = SSD_GROUPS * HEADS_PER_GROUP
SPLIT_WIDTH = 3 * SSD_HEADS

GMLP_CHUNK = 128
GMLP_GROUP_DIM = 128

RMS_EPS = 1e-6
LN_EPS = 1e-5

V7X_VMEM_LIMIT_BYTES = 56 * 1024 * 1024


def _bdot(a, b):
    return jnp.dot(a, b, preferred_element_type=F32)


def _rms(x, w, eps):
    return x * lax.rsqrt(jnp.mean(x * x, axis=-1, keepdims=True) + eps) * w


def _sigmoid(x):
    return 0.5 + 0.5 * jnp.tanh(0.5 * x)


def _silu(x):
    hx = 0.5 * x
    return hx + hx * jnp.tanh(hx)


def _gelu(x):
    return 0.5 * x * (1.0 + lax.erf(x * (1.0 / math.sqrt(2.0))))


def _softplus(x):
    return jnp.maximum(x, 0.0) + jnp.log1p(jnp.exp(-jnp.abs(x)))


def _bf16_pieces(x):
    hi = x.astype(BF16)
    r1 = x - hi.astype(F32)
    mid = r1.astype(BF16)
    lo = (r1 - mid.astype(F32)).astype(BF16)
    return hi, mid, lo


def _split_select(x3):
    hi, mid, lo = _bf16_pieces(x3)
    lane = lax.broadcasted_iota(jnp.int32, x3.shape, 1)
    return jnp.where(lane < SSD_HEADS, hi, jnp.where(lane < 2 * SSD_HEADS, mid, lo))


def _layer_spec(stacked, layer):
    index = (layer,) + (0,) * (stacked.ndim - 1)
    return pl.BlockSpec((None,) + stacked.shape[1:], lambda *_: index, pipeline_mode=pl.Buffered(1))


def _rows(stacked):
    return stacked.astype(F32).reshape(stacked.shape[0], 1, stacked.shape[1])


FFN_COL_BLOCK = 256


def _ffn_ple_kernel(h_ref, p_ref, nf_ref, wg_ref, wu_ref, wd_ref, wp_ref, pn_ref,
                    gn_ref, wpg_ref, fn_ref, o_ref, hn_scr, a_scr, *, final):
    h = h_ref[...]
    hn_scr[...] = _rms(h, nf_ref[...], RMS_EPS).astype(BF16)
    hn = hn_scr[...]
    ffn_dim = wg_ref.shape[1]
    for c in range(0, ffn_dim, FFN_COL_BLOCK):
        g = _bdot(hn, wg_ref[:, c:c + FFN_COL_BLOCK])
        u = _bdot(hn, wu_ref[:, c:c + FFN_COL_BLOCK])
        a_scr[:, c:c + FFN_COL_BLOCK] = (_silu(g) * u).astype(BF16)
    h = h + _bdot(a_scr[...], wd_ref[...])
    e = _rms(_bdot(p_ref[...].astype(BF16), wp_ref[...]), pn_ref[...], RMS_EPS)
    gate = _sigmoid(_bdot(_rms(h, gn_ref[...], RMS_EPS).astype(BF16), wpg_ref[...]))
    h = h + gate * e
    if final:
        h = _rms(h, fn_ref[...], RMS_EPS)
    o_ref[...] = h


def _ffn_ple_params(norm_ffn, w_gate, w_up, w_down, w_proj, ple_norm, gate_norm, w_pgate, final_norm):
    return (_rows(norm_ffn), w_gate.astype(BF16), w_up.astype(BF16), w_down.astype(BF16),
            w_proj.astype(BF16), _rows(ple_norm), _rows(gate_norm), w_pgate.astype(BF16),
            _rows(final_norm.reshape(1, -1)))


def _ffn_ple_layer(h, p, params, layer, *, final, tm):
    t, d = h.shape
    ffn_dim = params[1].shape[2]
    ple_dim = p.shape[2]
    assert t % tm == 0 and ffn_dim % FFN_COL_BLOCK == 0
    specs = [_layer_spec(v, layer) for v in params[:-1]] + [_layer_spec(params[-1], 0)]
    return pl.pallas_call(
        functools.partial(_ffn_ple_kernel, final=final),
        grid=(t // tm,),
        in_specs=[pl.BlockSpec((tm, d), lambda i: (i, 0)),
                  pl.BlockSpec((None, tm, ple_dim), lambda i: (layer, i, 0))] + specs,
        out_specs=pl.BlockSpec((tm, d), lambda i: (i, 0)),
        out_shape=jax.ShapeDtypeStruct((t, d), F32),
        scratch_shapes=[pltpu.VMEM((tm, d), BF16), pltpu.VMEM((tm, ffn_dim), BF16)],
        compiler_params=pltpu.CompilerParams(
            dimension_semantics=("arbitrary",), vmem_limit_bytes=V7X_VMEM_LIMIT_BYTES),
        name="ffn_ple",
    )(h, p, *params)


GMLP_COL_BLOCK = 256


def _gmlp_kernel(h_ref, nm_ref, win_ref, bin_ref, lnw_ref, lnb_ref, ws_ref, bs_ref, wout_ref,
                 o_ref, hn_scr, u_scr, v_scr, y_scr):
    tm = h_ref.shape[0]
    inner = u_scr.shape[1]
    n_chunks = tm // GMLP_CHUNK
    n_groups = inner // GMLP_GROUP_DIM
    hn_scr[...] = _rms(h_ref[...], nm_ref[...], RMS_EPS).astype(BF16)
    hn = hn_scr[...]
    for c in range(0, inner, GMLP_COL_BLOCK):
        cols = slice(inner + c, inner + c + GMLP_COL_BLOCK)
        v_scr[:, c:c + GMLP_COL_BLOCK] = _gelu(_bdot(hn, win_ref[:, cols]) + bin_ref[:, cols])
    v = v_scr[...]
    vc = v - jnp.mean(v, axis=-1, keepdims=True)
    vn = vc * lax.rsqrt(jnp.mean(vc * vc, axis=-1, keepdims=True) + LN_EPS)
    v_scr[...] = vn * lnw_ref[...] + lnb_ref[...]
    for c in range(0, inner, GMLP_COL_BLOCK):
        cols = slice(c, c + GMLP_COL_BLOCK)
        u_scr[:, cols] = _gelu(_bdot(hn, win_ref[:, cols]) + bin_ref[:, cols])
    causal = (lax.broadcasted_iota(jnp.int32, (GMLP_CHUNK, GMLP_CHUNK), 0)
              >= lax.broadcasted_iota(jnp.int32, (GMLP_CHUNK, GMLP_CHUNK), 1))
    for g in range(n_groups):
        cols = slice(g * GMLP_GROUP_DIM, (g + 1) * GMLP_GROUP_DIM)
        w = jnp.where(causal, ws_ref[g], 0.0).astype(BF16)
        rhs = jnp.concatenate(
            [v_scr[q * GMLP_CHUNK:(q + 1) * GMLP_CHUNK, cols].astype(BF16) for q in range(n_chunks)],
            axis=1)
        mixed = _bdot(w, rhs) + bs_ref[g]
        for q in range(n_chunks):
            rows = slice(q * GMLP_CHUNK, (q + 1) * GMLP_CHUNK)
            m = mixed[:, q * GMLP_GROUP_DIM:(q + 1) * GMLP_GROUP_DIM]
            y_scr[rows, cols] = (u_scr[rows, cols] * m).astype(BF16)
    o_ref[...] = h_ref[...] + _bdot(y_scr[...], wout_ref[...])


def _gmlp_params(w_in, b_in, ln_w, ln_b, w_s, b_s, w_out):
    assert w_s.shape[2:] == (GMLP_CHUNK, GMLP_CHUNK)
    assert w_s.shape[1] * GMLP_GROUP_DIM == w_out.shape[1]
    return (w_in.astype(BF16), _rows(b_in), _rows(ln_w), _rows(ln_b), w_s.astype(F32),
            b_s.astype(F32)[..., None], w_out.astype(BF16))


def _gmlp_layer(h, norm_rows, layer, params, mixer, *, tm):
    t, d = h.shape
    inner = params[-1].shape[1]
    assert t % tm == 0 and tm % GMLP_CHUNK == 0 and inner % GMLP_COL_BLOCK == 0
    tile = pl.BlockSpec((tm, d), lambda i: (i, 0))
    return pl.pallas_call(
        _gmlp_kernel,
        grid=(t // tm,),
        in_specs=[tile, _layer_spec(norm_rows, layer)] + [_layer_spec(v, mixer) for v in params],
        out_specs=tile,
        out_shape=jax.ShapeDtypeStruct((t, d), F32),
        scratch_shapes=[pltpu.VMEM((tm, d), BF16), pltpu.VMEM((tm, inner), F32),
                        pltpu.VMEM((tm, inner), F32), pltpu.VMEM((tm, inner), BF16)],
        compiler_params=pltpu.CompilerParams(
            dimension_semantics=("arbitrary",), vmem_limit_bytes=V7X_VMEM_LIMIT_BYTES),
        name="gmlp_mixer",
    )(h, norm_rows, *params)


XBC_GROUP_WIDTH = GROUP_WIDTH + 2 * SSD_STATE
SUBLANES = 8
LANES = 128
SEGMENT = SSD_CHUNK // SUBLANES
HALO_ROWS = (CONV_K - 1) * SUBLANES


def _ssd_kernel(h_ref, nm_ref, wz_ref, wxbc_ref, wdt_ref, wdtt_ref, cw_ref, cb_ref,
                dtb_ref, alog_ref, dtbc_ref, alogc_ref, dskip_ref, normw_ref, e64_ref, wout_ref,
                o_ref,
                stage_scr, hn_scr, halo_scr, zs_scr, xs_scr, b_scr, c_scr, cs_scr,
                mx_scr, csrow_scr, state_scr, yn_scr):
    tt = h_ref.shape[0]
    n_chunks = tt // SSD_CHUNK
    n_lane_tiles = h_ref.shape[1] // LANES
    step = pl.program_id(1)

    @pl.when(step == 0)
    def _():
        halo_scr[...] = jnp.zeros(halo_scr.shape, F32)
        state_scr[...] = jnp.zeros(state_scr.shape, F32)

    for c in range(n_lane_tiles):
        stage_scr[c] = h_ref[:, c * LANES:(c + 1) * LANES]
    for t in range(tt // (2 * SUBLANES)):
        q, i = divmod(2 * t, SEGMENT)
        tiles = [jnp.concatenate(
            [stage_scr[c, pl.ds(q * SSD_CHUNK + i + k, SUBLANES, stride=SEGMENT), :]
             for c in range(n_lane_tiles)], axis=1) for k in range(2)]
        row = 2 * SUBLANES * t
        hn_scr[row:row + 2 * SUBLANES, :] = _rms(
            jnp.concatenate(tiles, axis=0), nm_ref[...], RMS_EPS).astype(BF16)
    hn = hn_scr[...]

    dt = _softplus(_bdot(hn, wdt_ref[...]) + dtb_ref[...])
    da = dt * (-jnp.exp(alog_ref[...]))
    dt_rows = lax.dot_general(wdtt_ref[...], hn, (((1,), (1,)), ((), ())),
                              preferred_element_type=F32)

    def position(index):
        return (index % SUBLANES) * SEGMENT + index // SUBLANES

    pos_row = position(lax.broadcasted_iota(jnp.int32, (SSD_CHUNK, SSD_CHUNK), 0))
    pos_col = position(lax.broadcasted_iota(jnp.int32, (SSD_CHUNK, SSD_CHUNK), 1))
    causal = pos_row >= pos_col
    lower = jnp.where(causal, 1.0, 0.0).astype(BF16)
    upper = jnp.where(pos_row <= pos_col, 1.0, 0.0).astype(BF16)
    last_sublane = lax.broadcasted_iota(
        jnp.int32, (HALO_ROWS, XBC_GROUP_WIDTH), 0) % SUBLANES == SUBLANES - 1

    for q in range(n_chunks):
        rows = slice(q * SSD_CHUNK, (q + 1) * SSD_CHUNK)
        hi, mid, lo = _bf16_pieces(da[rows])
        cs = _bdot(lower, hi) + _bdot(lower, mid) + _bdot(lower, lo)
        total = cs[SSD_CHUNK - 1:SSD_CHUNK, :]
        cs_scr[q] = cs
        mx_scr[q, 0:SSD_CHUNK] = _split_select(dt[rows])
        mx_scr[q, SSD_CHUNK:2 * SSD_CHUNK] = _split_select(jnp.exp(total - cs))
        mx_scr[q, 2 * SSD_CHUNK:3 * SSD_CHUNK] = _split_select(jnp.exp(cs))
        dt_r = _softplus(dt_rows[:, rows] + dtbc_ref[...])
        da_r = dt_r * (-jnp.exp(alogc_ref[...]))
        hi, mid, lo = _bf16_pieces(da_r)
        csrow_scr[q] = _bdot(hi, upper) + _bdot(mid, upper) + _bdot(lo, upper)

    lane_head = lax.broadcasted_iota(jnp.int32, (SSD_CHUNK, GROUP_WIDTH), 1) // SSD_HEADDIM

    def project(g):
        cols = slice(g * XBC_GROUP_WIDTH, (g + 1) * XBC_GROUP_WIDTH)
        raw = _bdot(hn, wxbc_ref[:, cols])
        for q in range(n_chunks):
            rows = slice(q * SSD_CHUNK, (q + 1) * SSD_CHUNK)
            cur = raw[rows]
            tail = halo_scr[:, cols] if q == 0 else raw[q * SSD_CHUNK - HALO_ROWS:q * SSD_CHUNK]
            wrapped = jnp.where(last_sublane, tail, cur[SSD_CHUNK - HALO_ROWS:])
            boundary = [pltpu.roll(wrapped[k * SUBLANES:(k + 1) * SUBLANES], 1, axis=0)
                        for k in range(CONV_K - 1)]
            window = jnp.concatenate(boundary + [cur], axis=0)
            acc = cb_ref[:, cols] + cw_ref[CONV_K - 1:CONV_K, cols] * cur
            for k in range(CONV_K - 1):
                acc = acc + cw_ref[k:k + 1, cols] * window[k * SUBLANES:k * SUBLANES + SSD_CHUNK]
            xbc = _silu(acc)
            xs_scr[g, rows, :] = xbc[:, :GROUP_WIDTH]
            b_scr[g, rows, :] = xbc[:, GROUP_WIDTH:GROUP_WIDTH + SSD_STATE].astype(BF16)
            c_scr[g, rows, :] = xbc[:, GROUP_WIDTH + SSD_STATE:].astype(BF16)
        halo_scr[:, cols] = raw[tt - HALO_ROWS:]
        zs_scr[g] = _silu(_bdot(hn, wz_ref[:, g * GROUP_WIDTH:(g + 1) * GROUP_WIDTH]))

    def scan(g):
        e64 = e64_ref[g]
        for q in range(n_chunks):
            rows = slice(q * SSD_CHUNK, (q + 1) * SSD_CHUNK)
            ex = _bdot(mx_scr[q], e64)
            dt_x = ex[0:SSD_CHUNK]
            dec_state_x = ex[SSD_CHUNK:2 * SSD_CHUNK]
            dec_out_x = ex[2 * SSD_CHUNK:3 * SSD_CHUNK]
            xg = xs_scr[g, rows, :]
            bg = b_scr[g, rows, :]
            cg = c_scr[g, rows, :]
            xr = xg * dt_x
            xr_b = xr.astype(BF16)
            xd_b = (xr * dec_state_x).astype(BF16)
            cb = lax.dot_general(cg, bg, (((1,), (1,)), ((), ())), preferred_element_type=F32)
            w_parts, x_parts = [], []
            for r in range(HEADS_PER_GROUP):
                head = g * HEADS_PER_GROUP + r
                crow = csrow_scr[q, head:head + 1, :]
                ccol = jnp.broadcast_to(cs_scr[q, :, head:head + 1], (SSD_CHUNK, SSD_CHUNK))
                diff = ccol - crow
                decay = jnp.exp(jnp.where(causal, diff, -jnp.inf))
                w_parts.append((cb * decay).astype(BF16))
                x_parts.append(jnp.where(lane_head == r, xr_b, jnp.zeros_like(xr_b)))
            y_diag = _bdot(jnp.concatenate(w_parts, axis=1), jnp.concatenate(x_parts, axis=0))
            prev = state_scr[g]
            y_off = _bdot(cg, prev.astype(BF16)) * dec_out_x
            new_states = lax.dot_general(bg, xd_b, (((0,), (0,)), ((), ())),
                                         preferred_element_type=F32)
            state_scr[g] = prev * dec_out_x[SSD_CHUNK - 1:SSD_CHUNK, :] + new_states
            y = y_diag + y_off + xg * dskip_ref[g]
            gz = y * zs_scr[g, rows, :]
            gn = gz * lax.rsqrt(jnp.mean(gz * gz, axis=-1, keepdims=True) + LN_EPS)
            yn_scr[rows, g * GROUP_WIDTH:(g + 1) * GROUP_WIDTH] = (gn * normw_ref[g]).astype(BF16)

    lead = 2
    for g in range(lead):
        project(g)
    for g in range(SSD_GROUPS):
        if g + lead < SSD_GROUPS:
            project(g + lead)
        scan(g)

    mixed = _bdot(yn_scr[...], wout_ref[...])
    for q in range(n_chunks):
        for i in range(SEGMENT):
            row = q * SSD_CHUNK + SUBLANES * i
            for c in range(n_lane_tiles):
                stage_scr[c, pl.ds(q * SSD_CHUNK + i, SUBLANES, stride=SEGMENT), :] = (
                    mixed[row:row + SUBLANES, c * LANES:(c + 1) * LANES])
    for c in range(n_lane_tiles):
        o_ref[:, c * LANES:(c + 1) * LANES] = h_ref[:, c * LANES:(c + 1) * LANES] + stage_scr[c]


SSD_INNER = SSD_GROUPS * GROUP_WIDTH
SSD_BC_WIDTH = SSD_GROUPS * SSD_STATE
SSD_CONV_DIM = SSD_INNER + 2 * SSD_BC_WIDTH


def _ssd_params(w_in, conv_w, conv_b, dt_bias, a_log, d_skip, norm_w, w_out):
    n, d = w_in.shape[:2]
    assert w_in.shape[2] == 2 * SSD_INNER + 2 * SSD_BC_WIDTH + SSD_HEADS
    assert conv_w.shape[1:] == (CONV_K, SSD_CONV_DIM) and w_out.shape[1:] == (SSD_INNER, d)

    def by_group(v):
        lead = v.shape[:2]
        parts = (v[..., :SSD_INNER].reshape(lead + (SSD_GROUPS, GROUP_WIDTH)),
                 v[..., SSD_INNER:SSD_INNER + SSD_BC_WIDTH].reshape(lead + (SSD_GROUPS, SSD_STATE)),
                 v[..., SSD_INNER + SSD_BC_WIDTH:].reshape(lead + (SSD_GROUPS, SSD_STATE)))
        return jnp.concatenate(parts, axis=3).reshape(lead + (SSD_CONV_DIM,))

    w_z = w_in[..., :SSD_INNER].astype(BF16)
    w_xbc = by_group(w_in[..., SSD_INNER:SSD_INNER + SSD_CONV_DIM].astype(BF16))
    w_dt = w_in[..., SSD_INNER + SSD_CONV_DIM:].astype(BF16)
    tiled_rows = lambda v: jnp.tile(v.astype(F32).reshape(n, 1, SSD_HEADS), (1, 1, 3))
    columns = lambda v: jnp.broadcast_to(v.astype(F32)[..., None], (n, SSD_HEADS, SSD_CHUNK))
    piece_head = jnp.arange(SPLIT_WIDTH)[None, :, None] % SSD_HEADS
    lane_head = (jnp.arange(SSD_GROUPS)[:, None, None] * HEADS_PER_GROUP
                 + jnp.arange(GROUP_WIDTH)[None, None, :] // SSD_HEADDIM)
    e64 = jnp.broadcast_to((piece_head == lane_head).astype(BF16)[None],
                           (n, SSD_GROUPS, SPLIT_WIDTH, GROUP_WIDTH))
    return (w_z, w_xbc, jnp.tile(w_dt, (1, 1, 3)), jnp.swapaxes(w_dt, 1, 2),
            by_group(conv_w.astype(F32)), by_group(conv_b.astype(F32)[:, None, :]),
            tiled_rows(dt_bias), tiled_rows(a_log), columns(dt_bias), columns(a_log),
            jnp.repeat(d_skip.astype(F32), SSD_HEADDIM, axis=1).reshape(n, SSD_GROUPS, 1, GROUP_WIDTH),
            norm_w.astype(F32).reshape(n, SSD_GROUPS, 1, GROUP_WIDTH),
            e64, w_out.astype(BF16))


def _ssd_layer(h, norm_rows, layer, params, mixer, *, tt):
    b, s, d = h.shape
    inner, conv_dim = SSD_INNER, SSD_CONV_DIM
    assert s % tt == 0 and tt % SSD_CHUNK == 0
    n_chunks = tt // SSD_CHUNK
    tile = pl.BlockSpec((None, tt, d), lambda bi, i: (bi, i, 0))
    return pl.pallas_call(
        _ssd_kernel,
        grid=(b, s // tt),
        in_specs=[tile, _layer_spec(norm_rows, layer)] + [_layer_spec(v, mixer) for v in params],
        out_specs=tile,
        out_shape=jax.ShapeDtypeStruct((b, s, d), F32),
        scratch_shapes=[
            pltpu.VMEM((d // LANES, tt, LANES), F32),
            pltpu.VMEM((tt, d), BF16),
            pltpu.VMEM((HALO_ROWS, conv_dim), F32),
            pltpu.VMEM((SSD_GROUPS, tt, GROUP_WIDTH), F32),
            pltpu.VMEM((SSD_GROUPS, tt, GROUP_WIDTH), F32),
            pltpu.VMEM((SSD_GROUPS, tt, SSD_STATE), BF16),
            pltpu.VMEM((SSD_GROUPS, tt, SSD_STATE), BF16),
            pltpu.VMEM((n_chunks, SSD_CHUNK, SPLIT_WIDTH), F32),
            pltpu.VMEM((n_chunks, 3 * SSD_CHUNK, SPLIT_WIDTH), BF16),
            pltpu.VMEM((n_chunks, SSD_HEADS, SSD_CHUNK), F32),
            pltpu.VMEM((SSD_GROUPS, SSD_STATE, GROUP_WIDTH), F32),
            pltpu.VMEM((tt, inner), BF16),
        ],
        compiler_params=pltpu.CompilerParams(
            dimension_semantics=("arbitrary", "arbitrary"),
            vmem_limit_bytes=V7X_VMEM_LIMIT_BYTES),
        name="ssd_mixer",
    )(h, norm_rows, *params)


def _pick_tile(n, preferred):
    tile = min(n, preferred)
    assert n % tile == 0
    return tile


def kernel(x, p, norm_mix, norm_ffn, ssd_w_in, ssd_conv_w, ssd_conv_b, ssd_dt_bias, ssd_a_log, ssd_d, ssd_norm_w, ssd_w_out, gmlp_w_in, gmlp_b_in, gmlp_ln_w, gmlp_ln_b, gmlp_w_s, gmlp_b_s, gmlp_w_out, ffn_w_gate, ffn_w_up, ffn_w_down, ple_w_proj, ple_norm, ple_gate_norm, ple_w_gate, final_norm):
    b, s, d = x.shape
    t = b * s
    depth = norm_mix.shape[0]
    mix_norm_rows = _rows(norm_mix)
    ssd_params = _ssd_params(ssd_w_in, ssd_conv_w, ssd_conv_b, ssd_dt_bias, ssd_a_log, ssd_d,
                             ssd_norm_w, ssd_w_out)
    gmlp_params = _gmlp_params(gmlp_w_in, gmlp_b_in, gmlp_ln_w, gmlp_ln_b, gmlp_w_s, gmlp_b_s, gmlp_w_out)
    ffn_params = _ffn_ple_params(norm_ffn, ffn_w_gate, ffn_w_up, ffn_w_down, ple_w_proj, ple_norm,
                                 ple_gate_norm, ple_w_gate, final_norm)
    p = p.reshape(depth, t, p.shape[-1])
    h = x
    for i in range(depth):
        j = i // N_MIXERS
        if i % N_MIXERS == 0:
            h = _ssd_layer(h.reshape(b, s, d), mix_norm_rows, i, ssd_params, j, tt=_pick_tile(s, 512))
        else:
            h = _gmlp_layer(h.reshape(t, d), mix_norm_rows, i, gmlp_params, j, tm=_pick_tile(t, 512))
        h = _ffn_ple_layer(h.reshape(t, d), p, ffn_params, i, final=(i == depth - 1),
                           tm=_pick_tile(t, 512))
    return h.reshape(b, s, d)
```

```python
import functools
import math

import jax
import jax.numpy as jnp
from jax import lax
from jax.experimental import pallas as pl
from jax.experimental.pallas import tpu as pltpu

F32 = jnp.float32
BF16 = jnp.bfloat16

DEPTH = 4
N_MIXERS = 2

SSD_HEADDIM = 64
SSD_GROUPS = 8
SSD_STATE = 128
SSD_CHUNK = 128
CONV_K = 4
HEADS_PER_GROUP = 4
GROUP_WIDTH = HEADS_PER_GROUP * SSD_HEADDIM
SSD_HEADS = SSD_GROUPS * HEADS_PER_GROUP
SPLIT_WIDTH = 3 * SSD_HEADS

GMLP_CHUNK = 128
GMLP_GROUP_DIM = 128

RMS_EPS = 1e-6
LN_EPS = 1e-5

V7X_VMEM_LIMIT_BYTES = 56 * 1024 * 1024

def _bdot(a, b):
    return jnp.dot(a, b, preferred_element_type=F32)


def _rms(x, w, eps):
    return x * lax.rsqrt(jnp.mean(x * x, axis=-1, keepdims=True) + eps) * w


def _sigmoid(x):
    return 0.5 + 0.5 * jnp.tanh(0.5 * x)


def _silu(x):
    hx = 0.5 * x
    return hx + hx * jnp.tanh(hx)


def _gelu(x):
    return 0.5 * x * (1.0 + lax.erf(x * (1.0 / math.sqrt(2.0))))


def _softplus(x):
    return jnp.maximum(x, 0.0) + jnp.log1p(jnp.exp(-jnp.abs(x)))


def _bf16_pieces(x):
    hi = x.astype(BF16)
    r1 = x - hi.astype(F32)
    mid = r1.astype(BF16)
    lo = (r1 - mid.astype(F32)).astype(BF16)
    return hi, mid, lo


def _split_select(x3):
    hi, mid, lo = _bf16_pieces(x3)
    lane = lax.broadcasted_iota(jnp.int32, x3.shape, 1)
    return jnp.where(lane < SSD_HEADS, hi, jnp.where(lane < 2 * SSD_HEADS, mid, lo))


def _layer_spec(stacked, layer):
    index = (layer,) + (0,) * (stacked.ndim - 1)
    return pl.BlockSpec((None,) + stacked.shape[1:], lambda *_: index, pipeline_mode=pl.Buffered(1))


def _rows(stacked):
    return stacked.astype(F32).reshape(stacked.shape[0], 1, stacked.shape[1])


FFN_COL_BLOCK = 256


def _ffn_ple_kernel(h_ref, p_ref, nf_ref, wg_ref, wu_ref, wd_ref, wp_ref, pn_ref,
                    gn_ref, wpg_ref, fn_ref, o_ref, hn_scr, a_scr, *, final):
    h = h_ref[...]
    hn_scr[...] = _rms(h, nf_ref[...], RMS_EPS).astype(BF16)
    hn = hn_scr[...]
    ffn_dim = wg_ref.shape[1]
    for c in range(0, ffn_dim, FFN_COL_BLOCK):
        g = _bdot(hn, wg_ref[:, c:c + FFN_COL_BLOCK])
        u = _bdot(hn, wu_ref[:, c:c + FFN_COL_BLOCK])
        a_scr[:, c:c + FFN_COL_BLOCK] = (_silu(g) * u).astype(BF16)
    h = h + _bdot(a_scr[...], wd_ref[...])
    e = _rms(_bdot(p_ref[...].astype(BF16), wp_ref[...]), pn_ref[...], RMS_EPS)
    gate = _sigmoid(_bdot(_rms(h, gn_ref[...], RMS_EPS).astype(BF16), wpg_ref[...]))
    h = h + gate * e
    if final:
        h = _rms(h, fn_ref[...], RMS_EPS)
    o_ref[...] = h


def _ffn_ple_params(norm_ffn, w_gate, w_up, w_down, w_proj, ple_norm, gate_norm, w_pgate, final_norm):
    return (_rows(norm_ffn), w_gate.astype(BF16), w_up.astype(BF16), w_down.astype(BF16),
            w_proj.astype(BF16), _rows(ple_norm), _rows(gate_norm), w_pgate.astype(BF16),
            _rows(final_norm.reshape(1, -1)))


def _ffn_ple_layer(h, p, params, layer, *, final, tm):
    t, d = h.shape
    ffn_dim = params[1].shape[2]
    ple_dim = p.shape[2]
    assert t % tm == 0 and ffn_dim % FFN_COL_BLOCK == 0
    specs = [_layer_spec(v, layer) for v in params[:-1]] + [_layer_spec(params[-1], 0)]
    return pl.pallas_call(
        functools.partial(_ffn_ple_kernel, final=final),
        grid=(t // tm,),
        in_specs=[pl.BlockSpec((tm, d), lambda i: (i, 0)),
                  pl.BlockSpec((None, tm, ple_dim), lambda i: (layer, i, 0))] + specs,
        out_specs=pl.BlockSpec((tm, d), lambda i: (i, 0)),
        out_shape=jax.ShapeDtypeStruct((t, d), F32),
        scratch_shapes=[pltpu.VMEM((tm, d), BF16), pltpu.VMEM((tm, ffn_dim), BF16)],
        compiler_params=pltpu.CompilerParams(
            dimension_semantics=("arbitrary",), vmem_limit_bytes=V7X_VMEM_LIMIT_BYTES),
        name="ffn_ple",
    )(h, p, *params)


GMLP_COL_BLOCK = 256


def _gmlp_kernel(h_ref, nm_ref, win_ref, bin_ref, lnw_ref, lnb_ref, ws_ref, bs_ref, wout_ref,
                 o_ref, hn_scr, u_scr, v_scr, y_scr):
    tm = h_ref.shape[0]
    inner = u_scr.shape[1]
    n_chunks = tm // GMLP_CHUNK
    n_groups = inner // GMLP_GROUP_DIM
    hn_scr[...] = _rms(h_ref[...], nm_ref[...], RMS_EPS).astype(BF16)
    hn = hn_scr[...]
    for c in range(0, inner, GMLP_COL_BLOCK):
        cols = slice(inner + c, inner + c + GMLP_COL_BLOCK)
        v_scr[:, c:c + GMLP_COL_BLOCK] = _gelu(_bdot(hn, win_ref[:, cols]) + bin_ref[:, cols])
    v = v_scr[...]
    vc = v - jnp.mean(v, axis=-1, keepdims=True)
    vn = vc * lax.rsqrt(jnp.mean(vc * vc, axis=-1, keepdims=True) + LN_EPS)
    v_scr[...] = vn * lnw_ref[...] + lnb_ref[...]
    for c in range(0, inner, GMLP_COL_BLOCK):
        cols = slice(c, c + GMLP_COL_BLOCK)
        u_scr[:, cols] = _gelu(_bdot(hn, win_ref[:, cols]) + bin_ref[:, cols])
    causal = (lax.broadcasted_iota(jnp.int32, (GMLP_CHUNK, GMLP_CHUNK), 0)
              >= lax.broadcasted_iota(jnp.int32, (GMLP_CHUNK, GMLP_CHUNK), 1))
    for g in range(n_groups):
        cols = slice(g * GMLP_GROUP_DIM, (g + 1) * GMLP_GROUP_DIM)
        w = jnp.where(causal, ws_ref[g], 0.0).astype(BF16)
        rhs = jnp.concatenate(
            [v_scr[q * GMLP_CHUNK:(q + 1) * GMLP_CHUNK, cols].astype(BF16) for q in range(n_chunks)],
            axis=1)
        mixed = _bdot(w, rhs) + bs_ref[g]
        for q in range(n_chunks):
            rows = slice(q * GMLP_CHUNK, (q + 1) * GMLP_CHUNK)
            m = mixed[:, q * GMLP_GROUP_DIM:(q + 1) * GMLP_GROUP_DIM]
            y_scr[rows, cols] = (u_scr[rows, cols] * m).astype(BF16)
    o_ref[...] = h_ref[...] + _bdot(y_scr[...], wout_ref[...])


def _gmlp_params(w_in, b_in, ln_w, ln_b, w_s, b_s, w_out):
    assert w_s.shape[2:] == (GMLP_CHUNK, GMLP_CHUNK)
    assert w_s.shape[1] * GMLP_GROUP_DIM == w_out.shape[1]
    return (w_in.astype(BF16), _rows(b_in), _rows(ln_w), _rows(ln_b), w_s.astype(F32),
            b_s.astype(F32)[..., None], w_out.astype(BF16))


def _gmlp_layer(h, norm_rows, layer, params, mixer, *, tm):
    t, d = h.shape
    inner = params[-1].shape[1]
    assert t % tm == 0 and tm % GMLP_CHUNK == 0 and inner % GMLP_COL_BLOCK == 0
    tile = pl.BlockSpec((tm, d), lambda i: (i, 0))
    return pl.pallas_call(
        _gmlp_kernel,
        grid=(t // tm,),
        in_specs=[tile, _layer_spec(norm_rows, layer)] + [_layer_spec(v, mixer) for v in params],
        out_specs=tile,
        out_shape=jax.ShapeDtypeStruct((t, d), F32),
        scratch_shapes=[pltpu.VMEM((tm, d), BF16), pltpu.VMEM((tm, inner), F32),
                        pltpu.VMEM((tm, inner), F32), pltpu.VMEM((tm, inner), BF16)],
        compiler_params=pltpu.CompilerParams(
            dimension_semantics=("arbitrary",), vmem_limit_bytes=V7X_VMEM_LIMIT_BYTES),
        name="gmlp_mixer",
    )(h, norm_rows, *params)


XBC_GROUP_WIDTH = GROUP_WIDTH + 2 * SSD_STATE
SUBLANES = 8
LANES = 128
SEGMENT = SSD_CHUNK // SUBLANES
HALO_ROWS = (CONV_K - 1) * SUBLANES


def _ssd_kernel(h_ref, nm_ref, wz_ref, wxbc_ref, wdt_ref, wdtt_ref, cw_ref, cb_ref,
                dtb_ref, alog_ref, dtbc_ref, alogc_ref, dskip_ref, normw_ref, e64_ref, wout_ref,
                o_ref,
                stage_scr, hn_scr, halo_scr, zs_scr, xs_scr, b_scr, c_scr, cs_scr,
                mx_scr, csrow_scr, state_scr, yn_scr):
    tt = h_ref.shape[0]
    n_chunks = tt // SSD_CHUNK
    n_lane_tiles = h_ref.shape[1] // LANES
    step = pl.program_id(1)

    @pl.when(step == 0)
    def _():
        halo_scr[...] = jnp.zeros(halo_scr.shape, F32)
        state_scr[...] = jnp.zeros(state_scr.shape, F32)

    for c in range(n_lane_tiles):
        stage_scr[c] = h_ref[:, c * LANES:(c + 1) * LANES]
    for t in range(tt // (2 * SUBLANES)):
        q, i = divmod(2 * t, SEGMENT)
        tiles = [jnp.concatenate(
            [stage_scr[c, pl.ds(q * SSD_CHUNK + i + k, SUBLANES, stride=SEGMENT), :]
             for c in range(n_lane_tiles)], axis=1) for k in range(2)]
        row = 2 * SUBLANES * t
        hn_scr[row:row + 2 * SUBLANES, :] = _rms(
            jnp.concatenate(tiles, axis=0), nm_ref[...], RMS_EPS).astype(BF16)
    hn = hn_scr[...]

    dt = _softplus(_bdot(hn, wdt_ref[...]) + dtb_ref[...])
    da = dt * (-jnp.exp(alog_ref[...]))
    dt_rows = lax.dot_general(wdtt_ref[...], hn, (((1,), (1,)), ((), ())),
                              preferred_element_type=F32)

    def position(index):
        return (index % SUBLANES) * SEGMENT + index // SUBLANES

    pos_row = position(lax.broadcasted_iota(jnp.int32, (SSD_CHUNK, SSD_CHUNK), 0))
    pos_col = position(lax.broadcasted_iota(jnp.int32, (SSD_CHUNK, SSD_CHUNK), 1))
    causal = pos_row >= pos_col
    lower = jnp.where(causal, 1.0, 0.0).astype(BF16)
    upper = jnp.where(pos_row <= pos_col, 1.0, 0.0).astype(BF16)
    last_sublane = lax.broadcasted_iota(
        jnp.int32, (HALO_ROWS, XBC_GROUP_WIDTH), 0) % SUBLANES == SUBLANES - 1

    for q in range(n_chunks):
        rows = slice(q * SSD_CHUNK, (q + 1) * SSD_CHUNK)
        hi, mid, lo = _bf16_pieces(da[rows])
        cs = _bdot(lower, hi) + _bdot(lower, mid) + _bdot(lower, lo)
        total = cs[SSD_CHUNK - 1:SSD_CHUNK, :]
        cs_scr[q] = cs
        mx_scr[q, 0:SSD_CHUNK] = _split_select(dt[rows])
        mx_scr[q, SSD_CHUNK:2 * SSD_CHUNK] = _split_select(jnp.exp(total - cs))
        mx_scr[q, 2 * SSD_CHUNK:3 * SSD_CHUNK] = _split_select(jnp.exp(cs))
        dt_r = _softplus(dt_rows[:, rows] + dtbc_ref[...])
        da_r = dt_r * (-jnp.exp(alogc_ref[...]))
        hi, mid, lo = _bf16_pieces(da_r)
        csrow_scr[q] = _bdot(hi, upper) + _bdot(mid, upper) + _bdot(lo, upper)

    lane_head = lax.broadcasted_iota(jnp.int32, (SSD_CHUNK, GROUP_WIDTH), 1) // SSD_HEADDIM

    def project(g):
        cols = slice(g * XBC_GROUP_WIDTH, (g + 1) * XBC_GROUP_WIDTH)
        raw = _bdot(hn, wxbc_ref[:, cols])
        for q in range(n_chunks):
            rows = slice(q * SSD_CHUNK, (q + 1) * SSD_CHUNK)
            cur = raw[rows]
            tail = halo_scr[:, cols] if q == 0 else raw[q * SSD_CHUNK - HALO_ROWS:q * SSD_CHUNK]
            wrapped = jnp.where(last_sublane, tail, cur[SSD_CHUNK - HALO_ROWS:])
            boundary = [pltpu.roll(wrapped[k * SUBLANES:(k + 1) * SUBLANES], 1, axis=0)
                        for k in range(CONV_K - 1)]
            window = jnp.concatenate(boundary + [cur], axis=0)
            acc = cb_ref[:, cols] + cw_ref[CONV_K - 1:CONV_K, cols] * cur
            for k in range(CONV_K - 1):
                acc = acc + cw_ref[k:k + 1, cols] * window[k * SUBLANES:k * SUBLANES + SSD_CHUNK]
            xbc = _silu(acc)
            xs_scr[g, rows, :] = xbc[:, :GROUP_WIDTH]
            b_scr[g, rows, :] = xbc[:, GROUP_WIDTH:GROUP_WIDTH + SSD_STATE].astype(BF16)
            c_scr[g, rows, :] = xbc[:, GROUP_WIDTH + SSD_STATE:].astype(BF16)
        halo_scr[:, cols] = raw[tt - HALO_ROWS:]
        zs_scr[g] = _silu(_bdot(hn, wz_ref[:, g * GROUP_WIDTH:(g + 1) * GROUP_WIDTH]))

    def scan(g):
        e64 = e64_ref[g]
        for q in range(n_chunks):
            rows = slice(q * SSD_CHUNK, (q + 1) * SSD_CHUNK)
            ex = _bdot(mx_scr[q], e64)
            dt_x = ex[0:SSD_CHUNK]
            dec_state_x = ex[SSD_CHUNK:2 * SSD_CHUNK]
            dec_out_x = ex[2 * SSD_CHUNK:3 * SSD_CHUNK]
            xg = xs_scr[g, rows, :]
            bg = b_scr[g, rows, :]
            cg = c_scr[g, rows, :]
            xr = xg * dt_x
            xr_b = xr.astype(BF16)
            xd_b = (xr * dec_state_x).astype(BF16)
            cb = lax.dot_general(cg, bg, (((1,), (1,)), ((), ())), preferred_element_type=F32)
            w_parts, x_parts = [], []
            for r in range(HEADS_PER_GROUP):
                head = g * HEADS_PER_GROUP + r
                crow = csrow_scr[q, head:head + 1, :]
                ccol = jnp.broadcast_to(cs_scr[q, :, head:head + 1], (SSD_CHUNK, SSD_CHUNK))
                diff = ccol - crow
                decay = jnp.exp(jnp.where(causal, diff, -jnp.inf))
                w_parts.append((cb * decay).astype(BF16))
                x_parts.append(jnp.where(lane_head == r, xr_b, jnp.zeros_like(xr_b)))
            y_diag = _bdot(jnp.concatenate(w_parts, axis=1), jnp.concatenate(x_parts, axis=0))
            prev = state_scr[g]
            y_off = _bdot(cg, prev.astype(BF16)) * dec_out_x
            new_states = lax.dot_general(bg, xd_b, (((0,), (0,)), ((), ())),
                                         preferred_element_type=F32)
            state_scr[g] = prev * dec_out_x[SSD_CHUNK - 1:SSD_CHUNK, :] + new_states
            y = y_diag + y_off + xg * dskip_ref[g]
            gz = y * zs_scr[g, rows, :]
            gn = gz * lax.rsqrt(jnp.mean(gz * gz, axis=-1, keepdims=True) + LN_EPS)
            yn_scr[rows, g * GROUP_WIDTH:(g + 1) * GROUP_WIDTH] = (gn * normw_ref[g]).astype(BF16)

    lead = 1
    for g in range(lead):
        project(g)
    for g in range(SSD_GROUPS):
        if g + lead < SSD_GROUPS:
            project(g + lead)
        scan(g)

    mixed = _bdot(yn_scr[...], wout_ref[...])
    for q in range(n_chunks):
        for i in range(SEGMENT):
            row = q * SSD_CHUNK + SUBLANES * i
            for c in range(n_lane_tiles):
                stage_scr[c, pl.ds(q * SSD_CHUNK + i, SUBLANES, stride=SEGMENT), :] = (
                    mixed[row:row + SUBLANES, c * LANES:(c + 1) * LANES])
    for c in range(n_lane_tiles):
        o_ref[:, c * LANES:(c + 1) * LANES] = h_ref[:, c * LANES:(c + 1) * LANES] + stage_scr[c]


SSD_INNER = SSD_GROUPS * GROUP_WIDTH
SSD_BC_WIDTH = SSD_GROUPS * SSD_STATE
SSD_CONV_DIM = SSD_INNER + 2 * SSD_BC_WIDTH


def _ssd_params(w_in, conv_w, conv_b, dt_bias, a_log, d_skip, norm_w, w_out):
    n, d = w_in.shape[:2]
    assert w_in.shape[2] == 2 * SSD_INNER + 2 * SSD_BC_WIDTH + SSD_HEADS
    assert conv_w.shape[1:] == (CONV_K, SSD_CONV_DIM) and w_out.shape[1:] == (SSD_INNER, d)

    def by_group(v):
        lead = v.shape[:2]
        parts = (v[..., :SSD_INNER].reshape(lead + (SSD_GROUPS, GROUP_WIDTH)),
                 v[..., SSD_INNER:SSD_INNER + SSD_BC_WIDTH].reshape(lead + (SSD_GROUPS, SSD_STATE)),
                 v[..., SSD_INNER + SSD_BC_WIDTH:].reshape(lead + (SSD_GROUPS, SSD_STATE)))
        return jnp.concatenate(parts, axis=3).reshape(lead + (SSD_CONV_DIM,))

    w_z = w_in[..., :SSD_INNER].astype(BF16)
    w_xbc = by_group(w_in[..., SSD_INNER:SSD_INNER + SSD_CONV_DIM].astype(BF16))
    w_dt = w_in[..., SSD_INNER + SSD_CONV_DIM:].astype(BF16)
    tiled_rows = lambda v: jnp.tile(v.astype(F32).reshape(n, 1, SSD_HEADS), (1, 1, 3))
    columns = lambda v: jnp.broadcast_to(v.astype(F32)[..., None], (n, SSD_HEADS, SSD_CHUNK))
    piece_head = jnp.arange(SPLIT_WIDTH)[None, :, None] % SSD_HEADS
    lane_head = (jnp.arange(SSD_GROUPS)[:, None, None] * HEADS_PER_GROUP
                 + jnp.arange(GROUP_WIDTH)[None, None, :] // SSD_HEADDIM)
    e64 = jnp.broadcast_to((piece_head == lane_head).astype(BF16)[None],
                           (n, SSD_GROUPS, SPLIT_WIDTH, GROUP_WIDTH))
    return (w_z, w_xbc, jnp.tile(w_dt, (1, 1, 3)), jnp.swapaxes(w_dt, 1, 2),
            by_group(conv_w.astype(F32)), by_group(conv_b.astype(F32)[:, None, :]),
            tiled_rows(dt_bias), tiled_rows(a_log), columns(dt_bias), columns(a_log),
            jnp.repeat(d_skip.astype(F32), SSD_HEADDIM, axis=1).reshape(n, SSD_GROUPS, 1, GROUP_WIDTH),
            norm_w.astype(F32).reshape(n, SSD_GROUPS, 1, GROUP_WIDTH),
            e64, w_out.astype(BF16))


def _ssd_layer(h, norm_rows, layer, params, mixer, *, tt):
    b, s, d = h.shape
    inner, conv_dim = SSD_INNER, SSD_CONV_DIM
    assert s % tt == 0 and tt % SSD_CHUNK == 0
    n_chunks = tt // SSD_CHUNK
    tile = pl.BlockSpec((None, tt, d), lambda bi, i: (bi, i, 0))
    return pl.pallas_call(
        _ssd_kernel,
        grid=(b, s // tt),
        in_specs=[tile, _layer_spec(norm_rows, layer)] + [_layer_spec(v, mixer) for v in params],
        out_specs=tile,
        out_shape=jax.ShapeDtypeStruct((b, s, d), F32),
        scratch_shapes=[
            pltpu.VMEM((d // LANES, tt, LANES), F32),
            pltpu.VMEM((tt, d), BF16),
            pltpu.VMEM((HALO_ROWS, conv_dim), F32),
            pltpu.VMEM((SSD_GROUPS, tt, GROUP_WIDTH), F32),
            pltpu.VMEM((SSD_GROUPS, tt, GROUP_WIDTH), F32),
            pltpu.VMEM((SSD_GROUPS, tt, SSD_STATE), BF16),
            pltpu.VMEM((SSD_GROUPS, tt, SSD_STATE), BF16),
            pltpu.VMEM((n_chunks, SSD_CHUNK, SPLIT_WIDTH), F32),
            pltpu.VMEM((n_chunks, 3 * SSD_CHUNK, SPLIT_WIDTH), BF16),
            pltpu.VMEM((n_chunks, SSD_HEADS, SSD_CHUNK), F32),
            pltpu.VMEM((SSD_GROUPS, SSD_STATE, GROUP_WIDTH), F32),
            pltpu.VMEM((tt, inner), BF16),
        ],
        compiler_params=pltpu.CompilerParams(
            dimension_semantics=("arbitrary", "arbitrary"),
            vmem_limit_bytes=V7X_VMEM_LIMIT_BYTES),
        name="ssd_mixer",
    )(h, norm_rows, *params)


def _pick_tile(n, preferred):
    tile = min(n, preferred)
    assert n % tile == 0
    return tile


def kernel(x, p, norm_mix, norm_ffn, ssd_w_in, ssd_conv_w, ssd_conv_b, ssd_dt_bias, ssd_a_log, ssd_d, ssd_norm_w, ssd_w_out, gmlp_w_in, gmlp_b_in, gmlp_ln_w, gmlp_ln_b, gmlp_w_s, gmlp_b_s, gmlp_w_out, ffn_w_gate, ffn_w_up, ffn_w_down, ple_w_proj, ple_norm, ple_gate_norm, ple_w_gate, final_norm):
    b, s, d = x.shape
    t = b * s
    depth = norm_mix.shape[0]
    mix_norm_rows = _rows(norm_mix)
    ssd_params = _ssd_params(ssd_w_in, ssd_conv_w, ssd_conv_b, ssd_dt_bias, ssd_a_log, ssd_d,
                             ssd_norm_w, ssd_w_out)
    gmlp_params = _gmlp_params(gmlp_w_in, gmlp_b_in, gmlp_ln_w, gmlp_ln_b, gmlp_w_s, gmlp_b_s, gmlp_w_out)
    ffn_params = _ffn_ple_params(norm_ffn, ffn_w_gate, ffn_w_up, ffn_w_down, ple_w_proj, ple_norm,
                                 ple_gate_norm, ple_w_gate, final_norm)
    p = p.reshape(depth, t, p.shape[-1])
    h = x
    for i in range(depth):
        j = i // N_MIXERS
        if i % N_MIXERS == 0:
            h = _ssd_layer(h.reshape(b, s, d), mix_norm_rows, i, ssd_params, j, tt=_pick_tile(s, 512))
        else:
            h = _gmlp_layer(h.reshape(t, d), mix_norm_rows, i, gmlp_params, j, tm=_pick_tile(t, 512))
        h = _ffn_ple_layer(h.reshape(t, d), p, ffn_params, i, final=(i == depth - 1),
                           tm=_pick_tile(t, 1024))
    return h.reshape(b, s, d)
```

```python
import functools
import math

import jax
import jax.numpy as jnp
from jax import lax
from jax.experimental import pallas as pl
from jax.experimental.pallas import tpu as pltpu

F32 = jnp.float32
BF16 = jnp.bfloat16

DEPTH = 4
N_MIXERS = 2

SSD_HEADDIM = 64
SSD_GROUPS = 8
SSD_STATE = 128
SSD_CHUNK = 128
CONV_K = 4
HEADS_PER_GROUP = 4
GROUP_WIDTH = HEADS_PER_GROUP * SSD_HEADDIM
SSD_HEADS = SSD_GROUPS * HEADS_PER_GROUP
SPLIT_WIDTH = 3 * SSD_HEADS

GMLP_CHUNK = 128
GMLP_GROUP_DIM = 128

RMS_EPS = 1e-6
LN_EPS = 1e-5

V7X_VMEM_LIMIT_BYTES = 56 * 1024 * 1024

def _bdot(a, b):
    return jnp.dot(a, b, preferred_element_type=F32)


def _rms(x, w, eps):
    return x * lax.rsqrt(jnp.mean(x * x, axis=-1, keepdims=True) + eps) * w


def _sigmoid(x):
    return 0.5 + 0.5 * jnp.tanh(0.5 * x)


def _silu(x):
    hx = 0.5 * x
    return hx + hx * jnp.tanh(hx)


def _gelu(x):
    return 0.5 * x * (1.0 + lax.erf(x * (1.0 / math.sqrt(2.0))))


def _softplus(x):
    return jnp.maximum(x, 0.0) + jnp.log1p(jnp.exp(-jnp.abs(x)))


def _bf16_pieces(x):
    hi = x.astype(BF16)
    r1 = x - hi.astype(F32)
    mid = r1.astype(BF16)
    lo = (r1 - mid.astype(F32)).astype(BF16)
    return hi, mid, lo


def _split_select(x3):
    hi, mid, lo = _bf16_pieces(x3)
    lane = lax.broadcasted_iota(jnp.int32, x3.shape, 1)
    return jnp.where(lane < SSD_HEADS, hi, jnp.where(lane < 2 * SSD_HEADS, mid, lo))


def _layer_spec(stacked, layer):
    index = (layer,) + (0,) * (stacked.ndim - 1)
    return pl.BlockSpec((None,) + stacked.shape[1:], lambda *_: index, pipeline_mode=pl.Buffered(1))


def _rows(stacked):
    return stacked.astype(F32).reshape(stacked.shape[0], 1, stacked.shape[1])


FFN_COL_BLOCK = 256


def _ffn_ple_kernel(h_ref, p_ref, nf_ref, wg_ref, wu_ref, wd_ref, wp_ref, pn_ref,
                    gn_ref, wpg_ref, fn_ref, o_ref, hn_scr, a_scr, *, final):
    h = h_ref[...]
    hn_scr[...] = _rms(h, nf_ref[...], RMS_EPS).astype(BF16)
    hn = hn_scr[...]
    ffn_dim = wg_ref.shape[1]
    for c in range(0, ffn_dim, FFN_COL_BLOCK):
        g = _bdot(hn, wg_ref[:, c:c + FFN_COL_BLOCK])
        u = _bdot(hn, wu_ref[:, c:c + FFN_COL_BLOCK])
        a_scr[:, c:c + FFN_COL_BLOCK] = (_silu(g) * u).astype(BF16)
    h = h + _bdot(a_scr[...], wd_ref[...])
    e = _rms(_bdot(p_ref[...].astype(BF16), wp_ref[...]), pn_ref[...], RMS_EPS)
    gate = _sigmoid(_bdot(_rms(h, gn_ref[...], RMS_EPS).astype(BF16), wpg_ref[...]))
    h = h + gate * e
    if final:
        h = _rms(h, fn_ref[...], RMS_EPS)
    o_ref[...] = h


def _ffn_ple_params(norm_ffn, w_gate, w_up, w_down, w_proj, ple_norm, gate_norm, w_pgate, final_norm):
    return (_rows(norm_ffn), w_gate.astype(BF16), w_up.astype(BF16), w_down.astype(BF16),
            w_proj.astype(BF16), _rows(ple_norm), _rows(gate_norm), w_pgate.astype(BF16),
            _rows(final_norm.reshape(1, -1)))


def _ffn_ple_layer(h, p, params, layer, *, final, tm):
    t, d = h.shape
    ffn_dim = params[1].shape[2]
    ple_dim = p.shape[2]
    assert t % tm == 0 and ffn_dim % FFN_COL_BLOCK == 0
    specs = [_layer_spec(v, layer) for v in params[:-1]] + [_layer_spec(params[-1], 0)]
    return pl.pallas_call(
        functools.partial(_ffn_ple_kernel, final=final),
        grid=(t // tm,),
        in_specs=[pl.BlockSpec((tm, d), lambda i: (i, 0)),
                  pl.BlockSpec((None, tm, ple_dim), lambda i: (layer, i, 0))] + specs,
        out_specs=pl.BlockSpec((tm, d), lambda i: (i, 0)),
        out_shape=jax.ShapeDtypeStruct((t, d), F32),
        scratch_shapes=[pltpu.VMEM((tm, d), BF16), pltpu.VMEM((tm, ffn_dim), BF16)],
        compiler_params=pltpu.CompilerParams(
            dimension_semantics=("arbitrary",), vmem_limit_bytes=V7X_VMEM_LIMIT_BYTES),
        name="ffn_ple",
    )(h, p, *params)


GMLP_COL_BLOCK = 256


def _gmlp_kernel(h_ref, nm_ref, win_ref, bin_ref, lnw_ref, lnb_ref, ws_ref, bs_ref, wout_ref,
                 o_ref, hn_scr, u_scr, v_scr, y_scr):
    tm = h_ref.shape[0]
    inner = u_scr.shape[1]
    n_chunks = tm // GMLP_CHUNK
    n_groups = inner // GMLP_GROUP_DIM
    hn_scr[...] = _rms(h_ref[...], nm_ref[...], RMS_EPS).astype(BF16)
    hn = hn_scr[...]
    for c in range(0, inner, GMLP_COL_BLOCK):
        cols = slice(inner + c, inner + c + GMLP_COL_BLOCK)
        v_scr[:, c:c + GMLP_COL_BLOCK] = _gelu(_bdot(hn, win_ref[:, cols]) + bin_ref[:, cols])
    v = v_scr[...]
    vc = v - jnp.mean(v, axis=-1, keepdims=True)
    vn = vc * lax.rsqrt(jnp.mean(vc * vc, axis=-1, keepdims=True) + LN_EPS)
    v_scr[...] = vn * lnw_ref[...] + lnb_ref[...]
    for c in range(0, inner, GMLP_COL_BLOCK):
        cols = slice(c, c + GMLP_COL_BLOCK)
        u_scr[:, cols] = _gelu(_bdot(hn, win_ref[:, cols]) + bin_ref[:, cols])
    causal = (lax.broadcasted_iota(jnp.int32, (GMLP_CHUNK, GMLP_CHUNK), 0)
              >= lax.broadcasted_iota(jnp.int32, (GMLP_CHUNK, GMLP_CHUNK), 1))
    for g in range(n_groups):
        cols = slice(g * GMLP_GROUP_DIM, (g + 1) * GMLP_GROUP_DIM)
        w = jnp.where(causal, ws_ref[g], 0.0).astype(BF16)
        rhs = jnp.concatenate(
            [v_scr[q * GMLP_CHUNK:(q + 1) * GMLP_CHUNK, cols].astype(BF16) for q in range(n_chunks)],
            axis=1)
        mixed = _bdot(w, rhs) + bs_ref[g]
        for q in range(n_chunks):
            rows = slice(q * GMLP_CHUNK, (q + 1) * GMLP_CHUNK)
            m = mixed[:, q * GMLP_GROUP_DIM:(q + 1) * GMLP_GROUP_DIM]
            y_scr[rows, cols] = (u_scr[rows, cols] * m).astype(BF16)
    o_ref[...] = h_ref[...] + _bdot(y_scr[...], wout_ref[...])


def _gmlp_params(w_in, b_in, ln_w, ln_b, w_s, b_s, w_out):
    assert w_s.shape[2:] == (GMLP_CHUNK, GMLP_CHUNK)
    assert w_s.shape[1] * GMLP_GROUP_DIM == w_out.shape[1]
    return (w_in.astype(BF16), _rows(b_in), _rows(ln_w), _rows(ln_b), w_s.astype(F32),
            b_s.astype(F32)[..., None], w_out.astype(BF16))


def _gmlp_layer(h, norm_rows, layer, params, mixer, *, tm):
    t, d = h.shape
    inner = params[-1].shape[1]
    assert t % tm == 0 and tm % GMLP_CHUNK == 0 and inner % GMLP_COL_BLOCK == 0
    tile = pl.BlockSpec((tm, d), lambda i: (i, 0))
    return pl.pallas_call(
        _gmlp_kernel,
        grid=(t // tm,),
        in_specs=[tile, _layer_spec(norm_rows, layer)] + [_layer_spec(v, mixer) for v in params],
        out_specs=tile,
        out_shape=jax.ShapeDtypeStruct((t, d), F32),
        scratch_shapes=[pltpu.VMEM((tm, d), BF16), pltpu.VMEM((tm, inner), F32),
                        pltpu.VMEM((tm, inner), F32), pltpu.VMEM((tm, inner), BF16)],
        compiler_params=pltpu.CompilerParams(
            dimension_semantics=("arbitrary",), vmem_limit_bytes=V7X_VMEM_LIMIT_BYTES),
        name="gmlp_mixer",
    )(h, norm_rows, *params)


XBC_GROUP_WIDTH = GROUP_WIDTH + 2 * SSD_STATE
SUBLANES = 8
LANES = 128
SEGMENT = SSD_CHUNK // SUBLANES
HALO_ROWS = (CONV_K - 1) * SUBLANES


def _ssd_kernel(h_ref, nm_ref, wz_ref, wxbc_ref, wdt_ref, wdtt_ref, cw_ref, cb_ref,
                dtb_ref, alog_ref, dtbc_ref, alogc_ref, dskip_ref, normw_ref, e64_ref, wout_ref,
                o_ref,
                stage_scr, hn_scr, halo_scr, zs_scr, xs_scr, b_scr, c_scr, cs_scr,
                mx_scr, csrow_scr, state_scr, yn_scr):
    tt = h_ref.shape[0]
    n_chunks = tt // SSD_CHUNK
    n_lane_tiles = h_ref.shape[1] // LANES
    step = pl.program_id(1)

    @pl.when(step == 0)
    def _():
        halo_scr[...] = jnp.zeros(halo_scr.shape, F32)
        state_scr[...] = jnp.zeros(state_scr.shape, F32)

    for c in range(n_lane_tiles):
        stage_scr[c] = h_ref[:, c * LANES:(c + 1) * LANES]
    for t in range(tt // (2 * SUBLANES)):
        q, i = divmod(2 * t, SEGMENT)
        tiles = [jnp.concatenate(
            [stage_scr[c, pl.ds(q * SSD_CHUNK + i + k, SUBLANES, stride=SEGMENT), :]
             for c in range(n_lane_tiles)], axis=1) for k in range(2)]
        row = 2 * SUBLANES * t
        hn_scr[row:row + 2 * SUBLANES, :] = _rms(
            jnp.concatenate(tiles, axis=0), nm_ref[...], RMS_EPS).astype(BF16)
    hn = hn_scr[...]

    dt = _softplus(_bdot(hn, wdt_ref[...]) + dtb_ref[...])
    da = dt * (-jnp.exp(alog_ref[...]))
    dt_rows = lax.dot_general(wdtt_ref[...], hn, (((1,), (1,)), ((), ())),
                              preferred_element_type=F32)

    def position(index):
        return (index % SUBLANES) * SEGMENT + index // SUBLANES

    pos_row = position(lax.broadcasted_iota(jnp.int32, (SSD_CHUNK, SSD_CHUNK), 0))
    pos_col = position(lax.broadcasted_iota(jnp.int32, (SSD_CHUNK, SSD_CHUNK), 1))
    causal = pos_row >= pos_col
    lower = jnp.where(causal, 1.0, 0.0).astype(BF16)
    upper = jnp.where(pos_row <= pos_col, 1.0, 0.0).astype(BF16)
    last_sublane = lax.broadcasted_iota(
        jnp.int32, (HALO_ROWS, XBC_GROUP_WIDTH), 0) % SUBLANES == SUBLANES - 1

    for q in range(n_chunks):
        rows = slice(q * SSD_CHUNK, (q + 1) * SSD_CHUNK)
        hi, mid, lo = _bf16_pieces(da[rows])
        cs = _bdot(lower, hi) + _bdot(lower, mid) + _bdot(lower, lo)
        total = cs[SSD_CHUNK - 1:SSD_CHUNK, :]
        cs_scr[q] = cs
        mx_scr[q, 0:SSD_CHUNK] = _split_select(dt[rows])
        mx_scr[q, SSD_CHUNK:2 * SSD_CHUNK] = _split_select(jnp.exp(total - cs))
        mx_scr[q, 2 * SSD_CHUNK:3 * SSD_CHUNK] = _split_select(jnp.exp(cs))
        dt_r = _softplus(dt_rows[:, rows] + dtbc_ref[...])
        da_r = dt_r * (-jnp.exp(alogc_ref[...]))
        hi, mid, lo = _bf16_pieces(da_r)
        csrow_scr[q] = _bdot(hi, upper) + _bdot(mid, upper) + _bdot(lo, upper)

    lane_head = lax.broadcasted_iota(jnp.int32, (SSD_CHUNK, GROUP_WIDTH), 1) // SSD_HEADDIM

    def project(g):
        cols = slice(g * XBC_GROUP_WIDTH, (g + 1) * XBC_GROUP_WIDTH)
        raw = _bdot(hn, wxbc_ref[:, cols])
        for q in range(n_chunks):
            rows = slice(q * SSD_CHUNK, (q + 1) * SSD_CHUNK)
            cur = raw[rows]
            tail = halo_scr[:, cols] if q == 0 else raw[q * SSD_CHUNK - HALO_ROWS:q * SSD_CHUNK]
            wrapped = jnp.where(last_sublane, tail, cur[SSD_CHUNK - HALO_ROWS:])
            boundary = [pltpu.roll(wrapped[k * SUBLANES:(k + 1) * SUBLANES], 1, axis=0)
                        for k in range(CONV_K - 1)]
            window = jnp.concatenate(boundary + [cur], axis=0)
            acc = cb_ref[:, cols] + cw_ref[CONV_K - 1:CONV_K, cols] * cur
            for k in range(CONV_K - 1):
                acc = acc + cw_ref[k:k + 1, cols] * window[k * SUBLANES:k * SUBLANES + SSD_CHUNK]
            xbc = _silu(acc)
            xs_scr[g, rows, :] = xbc[:, :GROUP_WIDTH]
            b_scr[g, rows, :] = xbc[:, GROUP_WIDTH:GROUP_WIDTH + SSD_STATE].astype(BF16)
            c_scr[g, rows, :] = xbc[:, GROUP_WIDTH + SSD_STATE:].astype(BF16)
        halo_scr[:, cols] = raw[tt - HALO_ROWS:]
        zs_scr[g] = _silu(_bdot(hn, wz_ref[:, g * GROUP_WIDTH:(g + 1) * GROUP_WIDTH]))

    def scan(g):
        e64 = e64_ref[g]
        for q in range(n_chunks):
            rows = slice(q * SSD_CHUNK, (q + 1) * SSD_CHUNK)
            ex = _bdot(mx_scr[q], e64)
            dt_x = ex[0:SSD_CHUNK]
            dec_state_x = ex[SSD_CHUNK:2 * SSD_CHUNK]
            dec_out_x = ex[2 * SSD_CHUNK:3 * SSD_CHUNK]
            xg = xs_scr[g, rows, :]
            bg = b_scr[g, rows, :]
            cg = c_scr[g, rows, :]
            xr = xg * dt_x
            xr_b = xr.astype(BF16)
            xd_b = (xr * dec_state_x).astype(BF16)
            cb = lax.dot_general(cg, bg, (((1,), (1,)), ((), ())), preferred_element_type=F32)
            w_parts, x_parts = [], []
            for r in range(HEADS_PER_GROUP):
                head = g * HEADS_PER_GROUP + r
                crow = csrow_scr[q, head:head + 1, :]
                ccol = jnp.broadcast_to(cs_scr[q, :, head:head + 1], (SSD_CHUNK, SSD_CHUNK))
                diff = ccol - crow
                decay = jnp.exp(jnp.where(causal, diff, -jnp.inf))
                w_parts.append((cb * decay).astype(BF16))
                x_parts.append(jnp.where(lane_head == r, xr_b, jnp.zeros_like(xr_b)))
            y_diag = _bdot(jnp.concatenate(w_parts, axis=1), jnp.concatenate(x_parts, axis=0))
            prev = state_scr[g]
            y_off = _bdot(cg, prev.astype(BF16)) * dec_out_x
            new_states = lax.dot_general(bg, xd_b, (((0,), (0,)), ((), ())),
                                         preferred_element_type=F32)
            state_scr[g] = prev * dec_out_x[SSD_CHUNK - 1:SSD_CHUNK, :] + new_states
            y = y_diag + y_off + xg * dskip_ref[g]
            gz = y * zs_scr[g, rows, :]
            gn = gz * lax.rsqrt(jnp.mean(gz * gz, axis=-1, keepdims=True) + LN_EPS)
            yn_scr[rows, g * GROUP_WIDTH:(g + 1) * GROUP_WIDTH] = (gn * normw_ref[g]).astype(BF16)

    lead = 1
    for g in range(lead):
        project(g)
    for g in range(SSD_GROUPS):
        if g + lead < SSD_GROUPS:
            project(g + lead)
        scan(g)

    mixed = _bdot(yn_scr[...], wout_ref[...])
    for q in range(n_chunks):
        for i in range(SEGMENT):
            row = q * SSD_CHUNK + SUBLANES * i
            for c in range(n_lane_tiles):
                stage_scr[c, pl.ds(q * SSD_CHUNK + i, SUBLANES, stride=SEGMENT), :] = (
                    mixed[row:row + SUBLANES, c * LANES:(c + 1) * LANES])
    for c in range(n_lane_tiles):
        o_ref[:, c * LANES:(c + 1) * LANES] = h_ref[:, c * LANES:(c + 1) * LANES] + stage_scr[c]


SSD_INNER = SSD_GROUPS * GROUP_WIDTH
SSD_BC_WIDTH = SSD_GROUPS * SSD_STATE
SSD_CONV_DIM = SSD_INNER + 2 * SSD_BC_WIDTH


def _ssd_params(w_in, conv_w, conv_b, dt_bias, a_log, d_skip, norm_w, w_out):
    n, d = w_in.shape[:2]
    assert w_in.shape[2] == 2 * SSD_INNER + 2 * SSD_BC_WIDTH + SSD_HEADS
    assert conv_w.shape[1:] == (CONV_K, SSD_CONV_DIM) and w_out.shape[1:] == (SSD_INNER, d)

    def by_group(v):
        lead = v.shape[:2]
        parts = (v[..., :SSD_INNER].reshape(lead + (SSD_GROUPS, GROUP_WIDTH)),
                 v[..., SSD_INNER:SSD_INNER + SSD_BC_WIDTH].reshape(lead + (SSD_GROUPS, SSD_STATE)),
                 v[..., SSD_INNER + SSD_BC_WIDTH:].reshape(lead + (SSD_GROUPS, SSD_STATE)))
        return jnp.concatenate(parts, axis=3).reshape(lead + (SSD_CONV_DIM,))

    w_z = w_in[..., :SSD_INNER].astype(BF16)
    w_xbc = by_group(w_in[..., SSD_INNER:SSD_INNER + SSD_CONV_DIM].astype(BF16))
    w_dt = w_in[..., SSD_INNER + SSD_CONV_DIM:].astype(BF16)
    tiled_rows = lambda v: jnp.tile(v.astype(F32).reshape(n, 1, SSD_HEADS), (1, 1, 3))
    columns = lambda v: jnp.broadcast_to(v.astype(F32)[..., None], (n, SSD_HEADS, SSD_CHUNK))
    piece_head = jnp.arange(SPLIT_WIDTH)[None, :, None] % SSD_HEADS
    lane_head = (jnp.arange(SSD_GROUPS)[:, None, None] * HEADS_PER_GROUP
                 + jnp.arange(GROUP_WIDTH)[None, None, :] // SSD_HEADDIM)
    e64 = jnp.broadcast_to((piece_head == lane_head).astype(BF16)[None],
                           (n, SSD_GROUPS, SPLIT_WIDTH, GROUP_WIDTH))
    return (w_z, w_xbc, jnp.tile(w_dt, (1, 1, 3)), jnp.swapaxes(w_dt, 1, 2),
            by_group(conv_w.astype(F32)), by_group(conv_b.astype(F32)[:, None, :]),
            tiled_rows(dt_bias), tiled_rows(a_log), columns(dt_bias), columns(a_log),
            jnp.repeat(d_skip.astype(F32), SSD_HEADDIM, axis=1).reshape(n, SSD_GROUPS, 1, GROUP_WIDTH),
            norm_w.astype(F32).reshape(n, SSD_GROUPS, 1, GROUP_WIDTH),
            e64, w_out.astype(BF16))


def _ssd_layer(h, norm_rows, layer, params, mixer, *, tt):
    b, s, d = h.shape
    inner, conv_dim = SSD_INNER, SSD_CONV_DIM
    assert s % tt == 0 and tt % SSD_CHUNK == 0
    n_chunks = tt // SSD_CHUNK
    tile = pl.BlockSpec((None, tt, d), lambda bi, i: (bi, i, 0))
    return pl.pallas_call(
        _ssd_kernel,
        grid=(b, s // tt),
        in_specs=[tile, _layer_spec(norm_rows, layer)] + [_layer_spec(v, mixer) for v in params],
        out_specs=tile,
        out_shape=jax.ShapeDtypeStruct((b, s, d), F32),
        scratch_shapes=[
            pltpu.VMEM((d // LANES, tt, LANES), F32),
            pltpu.VMEM((tt, d), BF16),
            pltpu.VMEM((HALO_ROWS, conv_dim), F32),
            pltpu.VMEM((SSD_GROUPS, tt, GROUP_WIDTH), F32),
            pltpu.VMEM((SSD_GROUPS, tt, GROUP_WIDTH), F32),
            pltpu.VMEM((SSD_GROUPS, tt, SSD_STATE), BF16),
            pltpu.VMEM((SSD_GROUPS, tt, SSD_STATE), BF16),
            pltpu.VMEM((n_chunks, SSD_CHUNK, SPLIT_WIDTH), F32),
            pltpu.VMEM((n_chunks, 3 * SSD_CHUNK, SPLIT_WIDTH), BF16),
            pltpu.VMEM((n_chunks, SSD_HEADS, SSD_CHUNK), F32),
            pltpu.VMEM((SSD_GROUPS, SSD_STATE, GROUP_WIDTH), F32),
            pltpu.VMEM((tt, inner), BF16),
        ],
        compiler_params=pltpu.CompilerParams(
            dimension_semantics=("arbitrary", "arbitrary"),
            vmem_limit_bytes=V7X_VMEM_LIMIT_BYTES),
        name="ssd_mixer",
    )(h, norm_rows, *params)


def _pick_tile(n, preferred):
    tile = min(n, preferred)
    assert n % tile == 0
    return tile


def kernel(x, p, norm_mix, norm_ffn, ssd_w_in, ssd_conv_w, ssd_conv_b, ssd_dt_bias, ssd_a_log, ssd_d, ssd_norm_w, ssd_w_out, gmlp_w_in, gmlp_b_in, gmlp_ln_w, gmlp_ln_b, gmlp_w_s, gmlp_b_s, gmlp_w_out, ffn_w_gate, ffn_w_up, ffn_w_down, ple_w_proj, ple_norm, ple_gate_norm, ple_w_gate, final_norm):
    b, s, d = x.shape
    t = b * s
    depth = norm_mix.shape[0]
    mix_norm_rows = _rows(norm_mix)
    ssd_params = _ssd_params(ssd_w_in, ssd_conv_w, ssd_conv_b, ssd_dt_bias, ssd_a_log, ssd_d,
                             ssd_norm_w, ssd_w_out)
    gmlp_params = _gmlp_params(gmlp_w_in, gmlp_b_in, gmlp_ln_w, gmlp_ln_b, gmlp_w_s, gmlp_b_s, gmlp_w_out)
    ffn_params = _ffn_ple_params(norm_ffn, ffn_w_gate, ffn_w_up, ffn_w_down, ple_w_proj, ple_norm,
                                 ple_gate_norm, ple_w_gate, final_norm)
    p = p.reshape(depth, t, p.shape[-1])
    h = x
    for i in range(depth):
        j = i // N_MIXERS
        if i % N_MIXERS == 0:
            h = _ssd_layer(h.reshape(b, s, d), mix_norm_rows, i, ssd_params, j, tt=_pick_tile(s, 512))
        else:
            h = _gmlp_layer(h.reshape(t, d), mix_norm_rows, i, gmlp_params, j, tm=_pick_tile(t, 1024))
        h = _ffn_ple_layer(h.reshape(t, d), p, ffn_params, i, final=(i == depth - 1),
                           tm=_pick_tile(t, 1024))
    return h.reshape(b, s, d)
```

```python
import functools
import math

import jax
import jax.numpy as jnp
from jax import lax
from jax.experimental import pallas as pl
from jax.experimental.pallas import tpu as pltpu

F32 = jnp.float32
BF16 = jnp.bfloat16

DEPTH = 4
N_MIXERS = 2

SSD_HEADDIM = 64
SSD_GROUPS = 8
SSD_STATE = 128
SSD_CHUNK = 128
CONV_K = 4
HEADS_PER_GROUP = 4
GROUP_WIDTH = HEADS_PER_GROUP * SSD_HEADDIM
SSD_HEADS = SSD_GROUPS * HEADS_PER_GROUP
SPLIT_WIDTH = 3 * SSD_HEADS

GMLP_CHUNK = 128
GMLP_GROUP_DIM = 128

RMS_EPS = 1e-6
LN_EPS = 1e-5

V7X_VMEM_LIMIT_BYTES = 56 * 1024 * 1024

def _bdot(a, b):
    return jnp.dot(a, b, preferred_element_type=F32)


def _rms(x, w, eps):
    return x * lax.rsqrt(jnp.mean(x * x, axis=-1, keepdims=True) + eps) * w


def _sigmoid(x):
    return 0.5 + 0.5 * jnp.tanh(0.5 * x)


def _silu(x):
    hx = 0.5 * x
    return hx + hx * jnp.tanh(hx)


def _gelu(x):
    return 0.5 * x * (1.0 + lax.erf(x * (1.0 / math.sqrt(2.0))))


def _softplus(x):
    return jnp.maximum(x, 0.0) + jnp.log1p(jnp.exp(-jnp.abs(x)))


def _bf16_pieces(x):
    hi = x.astype(BF16)
    r1 = x - hi.astype(F32)
    mid = r1.astype(BF16)
    lo = (r1 - mid.astype(F32)).astype(BF16)
    return hi, mid, lo


def _split_select(x3):
    hi, mid, lo = _bf16_pieces(x3)
    lane = lax.broadcasted_iota(jnp.int32, x3.shape, 1)
    return jnp.where(lane < SSD_HEADS, hi, jnp.where(lane < 2 * SSD_HEADS, mid, lo))


def _layer_spec(stacked, layer):
    index = (layer,) + (0,) * (stacked.ndim - 1)
    return pl.BlockSpec((None,) + stacked.shape[1:], lambda *_: index, pipeline_mode=pl.Buffered(1))


def _rows(stacked):
    return stacked.astype(F32).reshape(stacked.shape[0], 1, stacked.shape[1])


FFN_COL_BLOCK = 256


def _ffn_ple_kernel(h_ref, p_ref, nf_ref, wg_ref, wu_ref, wd_ref, wp_ref, pn_ref,
                    gn_ref, wpg_ref, fn_ref, o_ref, hn_scr, a_scr, *, final):
    h = h_ref[...]
    hn_scr[...] = _rms(h, nf_ref[...], RMS_EPS).astype(BF16)
    hn = hn_scr[...]
    ffn_dim = wg_ref.shape[1]
    for c in range(0, ffn_dim, FFN_COL_BLOCK):
        g = _bdot(hn, wg_ref[:, c:c + FFN_COL_BLOCK])
        u = _bdot(hn, wu_ref[:, c:c + FFN_COL_BLOCK])
        a_scr[:, c:c + FFN_COL_BLOCK] = (_silu(g) * u).astype(BF16)
    h = h + _bdot(a_scr[...], wd_ref[...])
    e = _rms(_bdot(p_ref[...].astype(BF16), wp_ref[...]), pn_ref[...], RMS_EPS)
    gate = _sigmoid(_bdot(_rms(h, gn_ref[...], RMS_EPS).astype(BF16), wpg_ref[...]))
    h = h + gate * e
    if final:
        h = _rms(h, fn_ref[...], RMS_EPS)
    o_ref[...] = h


def _ffn_ple_params(norm_ffn, w_gate, w_up, w_down, w_proj, ple_norm, gate_norm, w_pgate, final_norm):
    return (_rows(norm_ffn), w_gate.astype(BF16), w_up.astype(BF16), w_down.astype(BF16),
            w_proj.astype(BF16), _rows(ple_norm), _rows(gate_norm), w_pgate.astype(BF16),
            _rows(final_norm.reshape(1, -1)))


def _ffn_ple_layer(h, p, params, layer, *, final, tm):
    t, d = h.shape
    ffn_dim = params[1].shape[2]
    ple_dim = p.shape[2]
    assert t % tm == 0 and ffn_dim % FFN_COL_BLOCK == 0
    specs = [_layer_spec(v, layer) for v in params[:-1]] + [_layer_spec(params[-1], 0)]
    return pl.pallas_call(
        functools.partial(_ffn_ple_kernel, final=final),
        grid=(t // tm,),
        in_specs=[pl.BlockSpec((tm, d), lambda i: (i, 0)),
                  pl.BlockSpec((None, tm, ple_dim), lambda i: (layer, i, 0))] + specs,
        out_specs=pl.BlockSpec((tm, d), lambda i: (i, 0)),
        out_shape=jax.ShapeDtypeStruct((t, d), F32),
        scratch_shapes=[pltpu.VMEM((tm, d), BF16), pltpu.VMEM((tm, ffn_dim), BF16)],
        compiler_params=pltpu.CompilerParams(
            dimension_semantics=("arbitrary",), vmem_limit_bytes=V7X_VMEM_LIMIT_BYTES),
        name="ffn_ple",
    )(h, p, *params)


GMLP_COL_BLOCK = 256


def _gmlp_kernel(h_ref, nm_ref, win_ref, bin_ref, lnw_ref, lnb_ref, ws_ref, bs_ref, wout_ref,
                 o_ref, hn_scr, u_scr, v_scr, y_scr):
    tm = h_ref.shape[0]
    inner = u_scr.shape[1]
    n_chunks = tm // GMLP_CHUNK
    n_groups = inner // GMLP_GROUP_DIM
    hn_scr[...] = _rms(h_ref[...], nm_ref[...], RMS_EPS).astype(BF16)
    hn = hn_scr[...]
    for c in range(0, inner, GMLP_COL_BLOCK):
        cols = slice(inner + c, inner + c + GMLP_COL_BLOCK)
        v_scr[:, c:c + GMLP_COL_BLOCK] = _gelu(_bdot(hn, win_ref[:, cols]) + bin_ref[:, cols])
    v = v_scr[...]
    vc = v - jnp.mean(v, axis=-1, keepdims=True)
    vn = vc * lax.rsqrt(jnp.mean(vc * vc, axis=-1, keepdims=True) + LN_EPS)
    v_scr[...] = vn * lnw_ref[...] + lnb_ref[...]
    for c in range(0, inner, GMLP_COL_BLOCK):
        cols = slice(c, c + GMLP_COL_BLOCK)
        u_scr[:, cols] = _gelu(_bdot(hn, win_ref[:, cols]) + bin_ref[:, cols])
    causal = (lax.broadcasted_iota(jnp.int32, (GMLP_CHUNK, GMLP_CHUNK), 0)
              >= lax.broadcasted_iota(jnp.int32, (GMLP_CHUNK, GMLP_CHUNK), 1))
    for g in range(n_groups):
        cols = slice(g * GMLP_GROUP_DIM, (g + 1) * GMLP_GROUP_DIM)
        w = jnp.where(causal, ws_ref[g], 0.0).astype(BF16)
        rhs = jnp.concatenate(
            [v_scr[q * GMLP_CHUNK:(q + 1) * GMLP_CHUNK, cols].astype(BF16) for q in range(n_chunks)],
            axis=1)
        mixed = _bdot(w, rhs) + bs_ref[g]
        for q in range(n_chunks):
            rows = slice(q * GMLP_CHUNK, (q + 1) * GMLP_CHUNK)
            m = mixed[:, q * GMLP_GROUP_DIM:(q + 1) * GMLP_GROUP_DIM]
            y_scr[rows, cols] = (u_scr[rows, cols] * m).astype(BF16)
    o_ref[...] = h_ref[...] + _bdot(y_scr[...], wout_ref[...])


def _gmlp_params(w_in, b_in, ln_w, ln_b, w_s, b_s, w_out):
    assert w_s.shape[2:] == (GMLP_CHUNK, GMLP_CHUNK)
    assert w_s.shape[1] * GMLP_GROUP_DIM == w_out.shape[1]
    return (w_in.astype(BF16), _rows(b_in), _rows(ln_w), _rows(ln_b), w_s.astype(F32),
            b_s.astype(F32)[..., None], w_out.astype(BF16))


def _gmlp_layer(h, norm_rows, layer, params, mixer, *, tm):
    t, d = h.shape
    inner = params[-1].shape[1]
    assert t % tm == 0 and tm % GMLP_CHUNK == 0 and inner % GMLP_COL_BLOCK == 0
    tile = pl.BlockSpec((tm, d), lambda i: (i, 0))
    return pl.pallas_call(
        _gmlp_kernel,
        grid=(t // tm,),
        in_specs=[tile, _layer_spec(norm_rows, layer)] + [_layer_spec(v, mixer) for v in params],
        out_specs=tile,
        out_shape=jax.ShapeDtypeStruct((t, d), F32),
        scratch_shapes=[pltpu.VMEM((tm, d), BF16), pltpu.VMEM((tm, inner), F32),
                        pltpu.VMEM((tm, inner), F32), pltpu.VMEM((tm, inner), BF16)],
        compiler_params=pltpu.CompilerParams(
            dimension_semantics=("arbitrary",), vmem_limit_bytes=V7X_VMEM_LIMIT_BYTES),
        name="gmlp_mixer",
    )(h, norm_rows, *params)


XBC_GROUP_WIDTH = GROUP_WIDTH + 2 * SSD_STATE
SUBLANES = 8
LANES = 128
SEGMENT = SSD_CHUNK // SUBLANES
HALO_ROWS = (CONV_K - 1) * SUBLANES


def _ssd_kernel(h_ref, nm_ref, wz_ref, wxbc_ref, wdt_ref, wdtt_ref, cw_ref, cb_ref,
                dtb_ref, alog_ref, dtbc_ref, alogc_ref, dskip_ref, normw_ref, e64_ref, wout_ref,
                o_ref,
                stage_scr, hn_scr, halo_scr, zs_scr, xs_scr, b_scr, c_scr, cs_scr,
                mx_scr, csrow_scr, state_scr, yn_scr):
    tt = h_ref.shape[0]
    n_chunks = tt // SSD_CHUNK
    n_lane_tiles = h_ref.shape[1] // LANES
    step = pl.program_id(1)

    @pl.when(step == 0)
    def _():
        halo_scr[...] = jnp.zeros(halo_scr.shape, F32)
        state_scr[...] = jnp.zeros(state_scr.shape, F32)

    for c in range(n_lane_tiles):
        stage_scr[c] = h_ref[:, c * LANES:(c + 1) * LANES]
    for t in range(tt // (2 * SUBLANES)):
        q, i = divmod(2 * t, SEGMENT)
        tiles = [jnp.concatenate(
            [stage_scr[c, pl.ds(q * SSD_CHUNK + i + k, SUBLANES, stride=SEGMENT), :]
             for c in range(n_lane_tiles)], axis=1) for k in range(2)]
        row = 2 * SUBLANES * t
        hn_scr[row:row + 2 * SUBLANES, :] = _rms(
            jnp.concatenate(tiles, axis=0), nm_ref[...], RMS_EPS).astype(BF16)
    hn = hn_scr[...]

    dt = _softplus(_bdot(hn, wdt_ref[...]) + dtb_ref[...])
    da = dt * (-jnp.exp(alog_ref[...]))
    dt_rows = lax.dot_general(wdtt_ref[...], hn, (((1,), (1,)), ((), ())),
                              preferred_element_type=F32)

    def position(index):
        return (index % SUBLANES) * SEGMENT + index // SUBLANES

    pos_row = position(lax.broadcasted_iota(jnp.int32, (SSD_CHUNK, SSD_CHUNK), 0))
    pos_col = position(lax.broadcasted_iota(jnp.int32, (SSD_CHUNK, SSD_CHUNK), 1))
    causal = pos_row >= pos_col
    lower = jnp.where(causal, 1.0, 0.0).astype(BF16)
    upper = jnp.where(pos_row <= pos_col, 1.0, 0.0).astype(BF16)
    last_sublane = lax.broadcasted_iota(
        jnp.int32, (HALO_ROWS, XBC_GROUP_WIDTH), 0) % SUBLANES == SUBLANES - 1

    for q in range(n_chunks):
        rows = slice(q * SSD_CHUNK, (q + 1) * SSD_CHUNK)
        hi, mid, lo = _bf16_pieces(da[rows])
        cs = _bdot(lower, hi) + _bdot(lower, mid) + _bdot(lower, lo)
        total = cs[SSD_CHUNK - 1:SSD_CHUNK, :]
        cs_scr[q] = cs
        mx_scr[q, 0:SSD_CHUNK] = _split_select(dt[rows])
        mx_scr[q, SSD_CHUNK:2 * SSD_CHUNK] = _split_select(jnp.exp(total - cs))
        mx_scr[q, 2 * SSD_CHUNK:3 * SSD_CHUNK] = _split_select(jnp.exp(cs))
        dt_r = _softplus(dt_rows[:, rows] + dtbc_ref[...])
        da_r = dt_r * (-jnp.exp(alogc_ref[...]))
        hi, mid, lo = _bf16_pieces(da_r)
        csrow_scr[q] = _bdot(hi, upper) + _bdot(mid, upper) + _bdot(lo, upper)

    lane_head = lax.broadcasted_iota(jnp.int32, (SSD_CHUNK, GROUP_WIDTH), 1) // SSD_HEADDIM

    def project(g):
        cols = slice(g * XBC_GROUP_WIDTH, (g + 1) * XBC_GROUP_WIDTH)
        raw = _bdot(hn, wxbc_ref[:, cols])
        for q in range(n_chunks):
            rows = slice(q * SSD_CHUNK, (q + 1) * SSD_CHUNK)
            cur = raw[rows]
            tail = halo_scr[:, cols] if q == 0 else raw[q * SSD_CHUNK - HALO_ROWS:q * SSD_CHUNK]
            wrapped = jnp.where(last_sublane, tail, cur[SSD_CHUNK - HALO_ROWS:])
            boundary = [pltpu.roll(wrapped[k * SUBLANES:(k + 1) * SUBLANES], 1, axis=0)
                        for k in range(CONV_K - 1)]
            window = jnp.concatenate(boundary + [cur], axis=0)
            acc = cb_ref[:, cols] + cw_ref[CONV_K - 1:CONV_K, cols] * cur
            for k in range(CONV_K - 1):
                acc = acc + cw_ref[k:k + 1, cols] * window[k * SUBLANES:k * SUBLANES + SSD_CHUNK]
            xbc = _silu(acc)
            xs_scr[g, rows, :] = xbc[:, :GROUP_WIDTH]
            b_scr[g, rows, :] = xbc[:, GROUP_WIDTH:GROUP_WIDTH + SSD_STATE].astype(BF16)
            c_scr[g, rows, :] = xbc[:, GROUP_WIDTH + SSD_STATE:].astype(BF16)
        halo_scr[:, cols] = raw[tt - HALO_ROWS:]
        zs_scr[g] = _silu(_bdot(hn, wz_ref[:, g * GROUP_WIDTH:(g + 1) * GROUP_WIDTH]))

    def scan(g):
        e64 = e64_ref[g]
        for q in range(n_chunks):
            rows = slice(q * SSD_CHUNK, (q + 1) * SSD_CHUNK)
            ex = _bdot(mx_scr[q], e64)
            dt_x = ex[0:SSD_CHUNK]
            dec_state_x = ex[SSD_CHUNK:2 * SSD_CHUNK]
            dec_out_x = ex[2 * SSD_CHUNK:3 * SSD_CHUNK]
            xg = xs_scr[g, rows, :]
            bg = b_scr[g, rows, :]
            cg = c_scr[g, rows, :]
            xr = xg * dt_x
            xr_b = xr.astype(BF16)
            xd_b = (xr * dec_state_x).astype(BF16)
            cb = lax.dot_general(cg, bg, (((1,), (1,)), ((), ())), preferred_element_type=F32)
            w_parts, x_parts = [], []
            for r in range(HEADS_PER_GROUP):
                head = g * HEADS_PER_GROUP + r
                crow = csrow_scr[q, head:head + 1, :]
                ccol = jnp.broadcast_to(cs_scr[q, :, head:head + 1], (SSD_CHUNK, SSD_CHUNK))
                diff = ccol - crow
                decay = jnp.exp(jnp.where(causal, diff, -jnp.inf))
                w_parts.append((cb * decay).astype(BF16))
                x_parts.append(jnp.where(lane_head == r, xr_b, jnp.zeros_like(xr_b)))
            y_diag = _bdot(jnp.concatenate(w_parts, axis=1), jnp.concatenate(x_parts, axis=0))
            prev = state_scr[g]
            y_off = _bdot(cg, prev.astype(BF16)) * dec_out_x
            new_states = lax.dot_general(bg, xd_b, (((0,), (0,)), ((), ())),
                                         preferred_element_type=F32)
            state_scr[g] = prev * dec_out_x[SSD_CHUNK - 1:SSD_CHUNK, :] + new_states
            y = y_diag + y_off + xg * dskip_ref[g]
            gz = y * zs_scr[g, rows, :]
            gn = gz * lax.rsqrt(jnp.mean(gz * gz, axis=-1, keepdims=True) + LN_EPS)
            yn_scr[rows, g * GROUP_WIDTH:(g + 1) * GROUP_WIDTH] = (gn * normw_ref[g]).astype(BF16)

    for g in range(SSD_GROUPS):
        project(g)
    for g in range(SSD_GROUPS):
        scan(g)

    mixed = _bdot(yn_scr[...], wout_ref[...])
    for q in range(n_chunks):
        for i in range(SEGMENT):
            row = q * SSD_CHUNK + SUBLANES * i
            for c in range(n_lane_tiles):
                stage_scr[c, pl.ds(q * SSD_CHUNK + i, SUBLANES, stride=SEGMENT), :] = (
                    mixed[row:row + SUBLANES, c * LANES:(c + 1) * LANES])
    for c in range(n_lane_tiles):
        o_ref[:, c * LANES:(c + 1) * LANES] = h_ref[:, c * LANES:(c + 1) * LANES] + stage_scr[c]


SSD_INNER = SSD_GROUPS * GROUP_WIDTH
SSD_BC_WIDTH = SSD_GROUPS * SSD_STATE
SSD_CONV_DIM = SSD_INNER + 2 * SSD_BC_WIDTH


def _ssd_params(w_in, conv_w, conv_b, dt_bias, a_log, d_skip, norm_w, w_out):
    n, d = w_in.shape[:2]
    assert w_in.shape[2] == 2 * SSD_INNER + 2 * SSD_BC_WIDTH + SSD_HEADS
    assert conv_w.shape[1:] == (CONV_K, SSD_CONV_DIM) and w_out.shape[1:] == (SSD_INNER, d)

    def by_group(v):
        lead = v.shape[:2]
        parts = (v[..., :SSD_INNER].reshape(lead + (SSD_GROUPS, GROUP_WIDTH)),
                 v[..., SSD_INNER:SSD_INNER + SSD_BC_WIDTH].reshape(lead + (SSD_GROUPS, SSD_STATE)),
                 v[..., SSD_INNER + SSD_BC_WIDTH:].reshape(lead + (SSD_GROUPS, SSD_STATE)))
        return jnp.concatenate(parts, axis=3).reshape(lead + (SSD_CONV_DIM,))

    w_z = w_in[..., :SSD_INNER].astype(BF16)
    w_xbc = by_group(w_in[..., SSD_INNER:SSD_INNER + SSD_CONV_DIM].astype(BF16))
    w_dt = w_in[..., SSD_INNER + SSD_CONV_DIM:].astype(BF16)
    tiled_rows = lambda v: jnp.tile(v.astype(F32).reshape(n, 1, SSD_HEADS), (1, 1, 3))
    columns = lambda v: jnp.broadcast_to(v.astype(F32)[..., None], (n, SSD_HEADS, SSD_CHUNK))
    piece_head = jnp.arange(SPLIT_WIDTH)[None, :, None] % SSD_HEADS
    lane_head = (jnp.arange(SSD_GROUPS)[:, None, None] * HEADS_PER_GROUP
                 + jnp.arange(GROUP_WIDTH)[None, None, :] // SSD_HEADDIM)
    e64 = jnp.broadcast_to((piece_head == lane_head).astype(BF16)[None],
                           (n, SSD_GROUPS, SPLIT_WIDTH, GROUP_WIDTH))
    return (w_z, w_xbc, jnp.tile(w_dt, (1, 1, 3)), jnp.swapaxes(w_dt, 1, 2),
            by_group(conv_w.astype(F32)), by_group(conv_b.astype(F32)[:, None, :]),
            tiled_rows(dt_bias), tiled_rows(a_log), columns(dt_bias), columns(a_log),
            jnp.repeat(d_skip.astype(F32), SSD_HEADDIM, axis=1).reshape(n, SSD_GROUPS, 1, GROUP_WIDTH),
            norm_w.astype(F32).reshape(n, SSD_GROUPS, 1, GROUP_WIDTH),
            e64, w_out.astype(BF16))


def _ssd_layer(h, norm_rows, layer, params, mixer, *, tt):
    b, s, d = h.shape
    inner, conv_dim = SSD_INNER, SSD_CONV_DIM
    assert s % tt == 0 and tt % SSD_CHUNK == 0
    n_chunks = tt // SSD_CHUNK
    tile = pl.BlockSpec((None, tt, d), lambda bi, i: (bi, i, 0))
    return pl.pallas_call(
        _ssd_kernel,
        grid=(b, s // tt),
        in_specs=[tile, _layer_spec(norm_rows, layer)] + [_layer_spec(v, mixer) for v in params],
        out_specs=tile,
        out_shape=jax.ShapeDtypeStruct((b, s, d), F32),
        scratch_shapes=[
            pltpu.VMEM((d // LANES, tt, LANES), F32),
            pltpu.VMEM((tt, d), BF16),
            pltpu.VMEM((HALO_ROWS, conv_dim), F32),
            pltpu.VMEM((SSD_GROUPS, tt, GROUP_WIDTH), F32),
            pltpu.VMEM((SSD_GROUPS, tt, GROUP_WIDTH), F32),
            pltpu.VMEM((SSD_GROUPS, tt, SSD_STATE), BF16),
            pltpu.VMEM((SSD_GROUPS, tt, SSD_STATE), BF16),
            pltpu.VMEM((n_chunks, SSD_CHUNK, SPLIT_WIDTH), F32),
            pltpu.VMEM((n_chunks, 3 * SSD_CHUNK, SPLIT_WIDTH), BF16),
            pltpu.VMEM((n_chunks, SSD_HEADS, SSD_CHUNK), F32),
            pltpu.VMEM((SSD_GROUPS, SSD_STATE, GROUP_WIDTH), F32),
            pltpu.VMEM((tt, inner), BF16),
        ],
        compiler_params=pltpu.CompilerParams(
            dimension_semantics=("arbitrary", "arbitrary"),
            vmem_limit_bytes=V7X_VMEM_LIMIT_BYTES),
        name="ssd_mixer",
    )(h, norm_rows, *params)


def _pick_tile(n, preferred):
    tile = min(n, preferred)
    assert n % tile == 0
    return tile


def kernel(x, p, norm_mix, norm_ffn, ssd_w_in, ssd_conv_w, ssd_conv_b, ssd_dt_bias, ssd_a_log, ssd_d, ssd_norm_w, ssd_w_out, gmlp_w_in, gmlp_b_in, gmlp_ln_w, gmlp_ln_b, gmlp_w_s, gmlp_b_s, gmlp_w_out, ffn_w_gate, ffn_w_up, ffn_w_down, ple_w_proj, ple_norm, ple_gate_norm, ple_w_gate, final_norm):
    b, s, d = x.shape
    t = b * s
    depth = norm_mix.shape[0]
    mix_norm_rows = _rows(norm_mix)
    ssd_params = _ssd_params(ssd_w_in, ssd_conv_w, ssd_conv_b, ssd_dt_bias, ssd_a_log, ssd_d,
                             ssd_norm_w, ssd_w_out)
    gmlp_params = _gmlp_params(gmlp_w_in, gmlp_b_in, gmlp_ln_w, gmlp_ln_b, gmlp_w_s, gmlp_b_s, gmlp_w_out)
    ffn_params = _ffn_ple_params(norm_ffn, ffn_w_gate, ffn_w_up, ffn_w_down, ple_w_proj, ple_norm,
                                 ple_gate_norm, ple_w_gate, final_norm)
    p = p.reshape(depth, t, p.shape[-1])
    h = x
    for i in range(depth):
        j = i // N_MIXERS
        if i % N_MIXERS == 0:
            h = _ssd_layer(h.reshape(b, s, d), mix_norm_rows, i, ssd_params, j, tt=_pick_tile(s, 512))
        else:
            h = _gmlp_layer(h.reshape(t, d), mix_norm_rows, i, gmlp_params, j, tm=_pick_tile(t, 1024))
        h = _ffn_ple_layer(h.reshape(t, d), p, ffn_params, i, final=(i == depth - 1),
                           tm=_pick_tile(t, 1024))
    return h.reshape(b, s, d)
```
